```python
import math
import jax, jax.numpy as jnp
from jax import lax
import numpy as np

D_MODEL = 2048
BATCH = 4
SEQ = 2048
DEPTH = 4

GRID_W = 64
CTX_LEN = 256
D_POOL = D_MODEL // 4
POOL_WINDOWS = (2, 4, 8, 16)
N_POOL_GROUPS = len(POOL_WINDOWS)
POOL_GROUP = D_POOL // N_POOL_GROUPS
D_HYENA = D_MODEL // 4
FILTER_EMB = 33
FILTER_BANDS = (FILTER_EMB - 1) // 2
FILTER_ORDER = 64
FILTER_DECAY_TARGET = 1e-2
FILTER_FAST_PCT = 0.3
FILTER_SLOW_PCT = 1.5
RET_HEAD_DIM = 256
D_RET = D_MODEL // 2
RET_HEADS = D_RET // RET_HEAD_DIM
RET_CHUNK = 128
ROPE_BASE = 10000.0
ROPE_PAIRS = RET_HEAD_DIM // 4
N_BRANCH = 3
D_FF = 4 * D_MODEL
LN_EPS = 1e-5
GN_EPS = 1e-6
DEEPNORM_ALPHA = (2 * DEPTH) ** 0.25
DEEPNORM_BETA = (8 * DEPTH) ** -0.25
O_POOL = 0
O_HY = O_POOL + D_POOL
O_Q = O_HY + 3 * D_HYENA
O_K = O_Q + D_RET
O_V = O_K + D_RET
O_G = O_V + D_RET
O_GATE = O_G + D_RET
D_IN = O_GATE + N_BRANCH * D_MODEL

kernel_name = 'hybrid_pool_hyena_retention_dit_block'


def layer_norm(x, g, b):
    xf = x.astype(jnp.float32)
    mu = xf.mean(-1, keepdims=True)
    var = jnp.square(xf - mu).mean(-1, keepdims=True)
    return ((xf - mu) * lax.rsqrt(var + LN_EPS) * g + b).astype(x.dtype)


def modulate(h, shift, scale):
    return h * (1.0 + scale) + shift


def pool_mixer(u, w, scale):
    B, L, _ = u.shape
    ug = u.astype(jnp.float32).reshape(B, L, N_POOL_GROUPS, POOL_GROUP)
    csum = jnp.concatenate([jnp.zeros_like(ug[:, :1]), jnp.cumsum(ug, axis=1)], axis=1)
    t = jnp.arange(L)[:, None]
    win = jnp.array(POOL_WINDOWS)[None, :]
    lo = jnp.clip(t - win // 2, 0, L)
    hi = jnp.clip(t + win - win // 2, 0, L)
    grp = jnp.arange(N_POOL_GROUPS)[None, :]
    wsum = csum[:, hi, grp] - csum[:, lo, grp]
    pooled = wsum / (hi - lo).astype(jnp.float32)[None, :, :, None] - ug
    y = jnp.einsum('blgc,gcd->blgd', pooled, w)
    return y.reshape(B, L, D_POOL) * scale


def short_conv(u, w, b):
    up = jnp.pad(u, ((0, 0), (1, 1), (0, 0)))
    return up[:, :-2] * w[0] + up[:, 1:-1] * w[1] + up[:, 2:] * w[2] + b


def hyena_filters(L, p):
    t = jnp.linspace(0.0, 1.0, L, dtype=jnp.float32)[:, None]
    w = 2.0 * math.pi * jnp.arange(L, dtype=jnp.float32)[:, None] / L
    f = jnp.linspace(1e-4, FILTER_BANDS - 1, FILTER_BANDS, dtype=jnp.float32)[None, :]
    z = jnp.concatenate([t, jnp.cos(f * w), -jnp.sin(f * w)], axis=-1)
    hdn = jnp.sin(p['filt_f1'] * (z @ p['filt_w1'] + p['filt_b1']))
    hdn = jnp.sin(p['filt_f2'] * (hdn @ p['filt_w2'] + p['filt_b2']))
    hdn = jnp.sin(p['filt_f3'] * (hdn @ p['filt_w3'] + p['filt_b3']))
    h = (hdn @ p['filt_w4']).astype(jnp.float32)
    max_decay = math.log(FILTER_DECAY_TARGET) / FILTER_FAST_PCT
    min_decay = math.log(FILTER_DECAY_TARGET) / FILTER_SLOW_PCT
    deltas = jnp.linspace(min_decay, max_decay, D_HYENA, dtype=jnp.float32)
    decay = jnp.exp(-t * jnp.abs(deltas)[None, :])
    h = h * jnp.concatenate([decay, decay], axis=-1)
    return h[:, :D_HYENA], h[:, D_HYENA:]


def bidir_long_conv(u, h_f, h_b):
    L = u.shape[1]
    k2 = jnp.concatenate([h_f, jnp.zeros_like(h_f[:1]), h_b[:0:-1]], axis=0)
    U = jnp.fft.rfft(u, n=2 * L, axis=1)
    K = jnp.fft.rfft(k2, n=2 * L, axis=0)
    return jnp.fft.irfft(U * K[None], n=2 * L, axis=1)[:, :L]


def hyena_mixer(u, p):
    L = u.shape[1]
    z = short_conv(u, p['conv_w'], p['conv_b']).astype(jnp.float32)
    v, x0, x1 = jnp.split(z, 3, axis=-1)
    h_f, h_b = hyena_filters(L, p)
    uu = v * x1
    return (bidir_long_conv(uu, h_f, h_b) + uu * p['hyena_d']) * x0


def grid_rope_tables(L):
    rows = L // GRID_W
    row = jnp.repeat(jnp.arange(rows, dtype=jnp.float32), GRID_W)
    col = jnp.tile(jnp.arange(GRID_W, dtype=jnp.float32), rows)
    inv = ROPE_BASE ** (-jnp.arange(ROPE_PAIRS, dtype=jnp.float32) / ROPE_PAIRS)
    ang_r = row[:, None] * inv[None, :]
    ang_c = col[:, None] * inv[None, :]
    return (jnp.cos(ang_r), jnp.sin(ang_r), jnp.cos(ang_c), jnp.sin(ang_c))


def rotate(x, cos, sin):
    x1, x2 = jnp.split(x, 2, axis=-1)
    return jnp.concatenate([x1 * cos - x2 * sin, x2 * cos + x1 * sin], axis=-1)


def apply_grid_rope(x, rope):
    cr, sr, cc, sc = rope
    half = RET_HEAD_DIM // 2
    return jnp.concatenate([rotate(x[..., :half], cr, sr), rotate(x[..., half:], cc, sc)], axis=-1)


def to_heads(t):
    B, L, _ = t.shape
    return t.reshape(B, L, RET_HEADS, RET_HEAD_DIM).transpose(0, 2, 1, 3).astype(jnp.float32)


def log_decays(param):
    lg = jnp.log1p(-jnp.exp(param.astype(jnp.float32)))
    return lg[0], lg[1]


def chunk_retention(q, k, v, lg, state0):
    B, H, L, _ = q.shape
    dv = v.shape[-1]
    n = L // RET_CHUNK

    def chunks(t):
        return t.reshape(B, H, n, RET_CHUNK, t.shape[-1]).transpose(2, 0, 1, 3, 4)

    idx = jnp.arange(RET_CHUNK, dtype=jnp.float32)
    rel = idx[:, None] - idx[None, :]
    lower = rel >= 0
    dmask = jnp.where(lower[None], jnp.exp(jnp.where(lower, rel, 0.0)[None] * lg[:, None, None]), 0.0)
    q_dec = jnp.exp((idx + 1.0)[None, :] * lg[:, None])[None, :, :, None]
    k_dec = jnp.exp((RET_CHUNK - 1.0 - idx)[None, :] * lg[:, None])[None, :, :, None]
    c_dec = jnp.exp(RET_CHUNK * lg)[None, :, None, None]

    def step(state, xs):
        qc, kc, vc = xs
        scores = jnp.einsum('bhid,bhjd->bhij', qc, kc) * dmask
        out = (jnp.einsum('bhij,bhjv->bhiv', scores, vc)
               + jnp.einsum('bhid,bhdv->bhiv', qc * q_dec, state))
        state = state * c_dec + jnp.einsum('bhjd,bhjv->bhdv', kc * k_dec, vc)
        return state, out

    _, out = lax.scan(step, state0, (chunks(q), chunks(k), chunks(v)))
    return out.transpose(1, 2, 0, 3, 4).reshape(B, H, L, dv)


def bidir_retention(q, k, v, lg_f, lg_b, s_f, s_b):
    def flip(t):
        return jnp.flip(t, axis=2)
    o_f = chunk_retention(q, k, v, lg_f, s_f)
    o_b = flip(chunk_retention(flip(q), flip(k), flip(v), lg_b, s_b))
    return o_f + o_b


def context_states(k, v, lg_f, lg_b):
    L = k.shape[2]
    pos = jnp.arange(L, dtype=jnp.float32)
    w_f = jnp.exp((L - 1.0 - pos)[None, :] * lg_f[:, None])
    w_b = jnp.exp(pos[None, :] * lg_b[:, None])
    s_f = jnp.einsum('bhlk,hl,bhlv->bhkv', k, w_f, v)
    s_b = jnp.einsum('bhlk,hl,bhlv->bhkv', k, w_b, v)
    return s_f, s_b


def hybrid_mixer(h, p, rope, s_f, s_b):
    B, L, _ = h.shape
    z = h @ p['w_in'] + p['b_in']
    y_a = pool_mixer(z[..., O_POOL:O_HY], p['pool_w'], p['pool_scale'])
    y_b = hyena_mixer(z[..., O_HY:O_Q], p)
    q = to_heads(z[..., O_Q:O_K])
    k = to_heads(z[..., O_K:O_V]) * RET_HEAD_DIM ** -0.5
    v = to_heads(z[..., O_V:O_G])
    if rope is not None:
        q = apply_grid_rope(q, rope)
        k = apply_grid_rope(k, rope)
    lg_f, lg_b = log_decays(p['ret_decay'])
    o = bidir_retention(q, k, v, lg_f, lg_b, s_f, s_b)
    mu = o.mean(-1, keepdims=True)
    var = jnp.square(o - mu).mean(-1, keepdims=True)
    o = ((o - mu) * lax.rsqrt(var + GN_EPS)).transpose(0, 2, 1, 3).reshape(B, L, D_RET)
    y_c = jax.nn.silu(z[..., O_G:O_GATE].astype(jnp.float32)) * o
    gates = jax.nn.sigmoid(z[..., O_GATE:].astype(jnp.float32)).reshape(B, L, N_BRANCH, D_MODEL)
    merged = (gates[:, :, 0] * (y_a @ p['p_a'])
              + gates[:, :, 1] * (y_b @ p['p_b'])
              + gates[:, :, 2] * (y_c @ p['p_c']))
    return merged @ p['w_o'] + p['b_o'], k, v


def context_kv(h, p):
    z = h @ p['w_in'][:, O_K:O_G] + p['b_in'][O_K:O_G]
    return to_heads(z[..., :D_RET]) * RET_HEAD_DIM ** -0.5, to_heads(z[..., D_RET:])


def sq_relu_mlp(h, p):
    a = jax.nn.relu(h @ p['w_mlp1'] + p['b_mlp1'])
    return jnp.square(a) @ p['w_mlp2'] + p['b_mlp2']


def setup_inputs(seed: int = 0) -> dict:
    key = jax.random.key(seed)
    ks = jax.random.split(key, 64)
    counter = [0]

    def nrm(shape, scale):
        k = ks[counter[0]]
        counter[0] += 1
        return jax.random.normal(k, shape, jnp.float32) * scale

    beta = DEEPNORM_BETA
    ret_base = -(5.0 + jnp.arange(RET_HEADS, dtype=jnp.float32)) * math.log(2.0)
    return {
        'x': nrm((BATCH, SEQ, D_MODEL), 1.0),
        'c': nrm((BATCH, D_MODEL), 1.0),
        'ctx': nrm((BATCH, CTX_LEN, D_MODEL), 1.0),
        'c_ctx': nrm((D_MODEL,), 1.0),
        'w_ada': nrm((DEPTH, D_MODEL, 6 * D_MODEL), 0.5 * D_MODEL ** -0.5),
        'b_ada': nrm((DEPTH, 6 * D_MODEL), 0.02),
        'w_in': nrm((DEPTH, D_MODEL, D_IN), D_MODEL ** -0.5),
        'b_in': nrm((DEPTH, D_IN), 0.02),
        'conv_w': nrm((DEPTH, 3, 3 * D_HYENA), 3.0 ** -0.5),
        'conv_b': nrm((DEPTH, 3 * D_HYENA), 0.02),
        'pool_w': nrm((DEPTH, N_POOL_GROUPS, POOL_GROUP, POOL_GROUP), POOL_GROUP ** -0.5),
        'pool_scale': 1.0 + nrm((DEPTH, D_POOL), 0.02),
        'filt_w1': nrm((DEPTH, FILTER_EMB, FILTER_ORDER), FILTER_EMB ** -0.5),
        'filt_b1': nrm((DEPTH, FILTER_ORDER), 0.02),
        'filt_f1': 1.0 + nrm((DEPTH, FILTER_ORDER), 0.1),
        'filt_w2': nrm((DEPTH, FILTER_ORDER, FILTER_ORDER), FILTER_ORDER ** -0.5),
        'filt_b2': nrm((DEPTH, FILTER_ORDER), 0.02),
        'filt_f2': 1.0 + nrm((DEPTH, FILTER_ORDER), 0.1),
        'filt_w3': nrm((DEPTH, FILTER_ORDER, FILTER_ORDER), FILTER_ORDER ** -0.5),
        'filt_b3': nrm((DEPTH, FILTER_ORDER), 0.02),
        'filt_f3': 1.0 + nrm((DEPTH, FILTER_ORDER), 0.1),
        'filt_w4': nrm((DEPTH, FILTER_ORDER, 2 * D_HYENA), 0.1 * FILTER_ORDER ** -0.5),
        'hyena_d': nrm((DEPTH, D_HYENA), 0.5),
        'ret_decay': ret_base + nrm((DEPTH, 2, RET_HEADS), 0.05),
        'p_a': nrm((DEPTH, D_POOL, D_MODEL), beta * D_POOL ** -0.5),
        'p_b': nrm((DEPTH, D_HYENA, D_MODEL), beta * D_HYENA ** -0.5),
        'p_c': nrm((DEPTH, D_RET, D_MODEL), beta * D_RET ** -0.5),
        'w_o': nrm((DEPTH, D_MODEL, D_MODEL), beta * D_MODEL ** -0.5),
        'b_o': nrm((DEPTH, D_MODEL), 0.02),
        'ln1_g': 1.0 + nrm((DEPTH, D_MODEL), 0.02),
        'ln1_b': nrm((DEPTH, D_MODEL), 0.02),
        'w_mlp1': nrm((DEPTH, D_MODEL, D_FF), D_MODEL ** -0.5),
        'b_mlp1': nrm((DEPTH, D_FF), 0.02),
        'w_mlp2': nrm((DEPTH, D_FF, D_MODEL), beta * D_FF ** -0.5),
        'b_mlp2': nrm((DEPTH, D_MODEL), 0.02),
        'ln2_g': 1.0 + nrm((DEPTH, D_MODEL), 0.02),
        'ln2_b': nrm((DEPTH, D_MODEL), 0.02),
    }


def reference(x, c, ctx, c_ctx, w_ada, b_ada, w_in, b_in, conv_w, conv_b, pool_w, pool_scale,
              filt_w1, filt_b1, filt_f1, filt_w2, filt_b2, filt_f2, filt_w3, filt_b3, filt_f3, filt_w4,
              hyena_d, ret_decay, p_a, p_b, p_c, w_o, b_o, ln1_g, ln1_b,
              w_mlp1, b_mlp1, w_mlp2, b_mlp2, ln2_g, ln2_b):
    L = x.shape[1]
    rope = grid_rope_tables(L)
    silu_c = jax.nn.silu(c)
    silu_cc = jax.nn.silu(c_ctx)
    zero_state = jnp.zeros((ctx.shape[0], RET_HEADS, RET_HEAD_DIM, RET_HEAD_DIM), jnp.float32)
    for l in range(DEPTH):
        p = {
            'w_in': w_in[l], 'b_in': b_in[l], 'conv_w': conv_w[l], 'conv_b': conv_b[l],
            'pool_w': pool_w[l], 'pool_scale': pool_scale[l],
            'filt_w1': filt_w1[l], 'filt_b1': filt_b1[l], 'filt_f1': filt_f1[l],
            'filt_w2': filt_w2[l], 'filt_b2': filt_b2[l], 'filt_f2': filt_f2[l],
            'filt_w3': filt_w3[l], 'filt_b3': filt_b3[l], 'filt_f3': filt_f3[l],
            'filt_w4': filt_w4[l], 'hyena_d': hyena_d[l], 'ret_decay': ret_decay[l],
            'p_a': p_a[l], 'p_b': p_b[l], 'p_c': p_c[l], 'w_o': w_o[l], 'b_o': b_o[l],
            'w_mlp1': w_mlp1[l], 'b_mlp1': b_mlp1[l], 'w_mlp2': w_mlp2[l], 'b_mlp2': b_mlp2[l],
        }
        mod_x = (silu_c @ w_ada[l] + b_ada[l])[:, None, :]
        mod_c = (silu_cc @ w_ada[l] + b_ada[l])[None, None, :]
        sh1, sc1, g1, sh2, sc2, g2 = jnp.split(mod_x, 6, axis=-1)
        csh1, csc1, cg1, csh2, csc2, cg2 = jnp.split(mod_c, 6, axis=-1)
        last = l == DEPTH - 1
        hc = modulate(ctx, csh1, csc1)
        if not last:
            yc, kc, vc = hybrid_mixer(hc, p, None, zero_state, zero_state)
        else:
            kc, vc = context_kv(hc, p)
        lg_f, lg_b = log_decays(p['ret_decay'])
        s_f, s_b = context_states(kc, vc, lg_f, lg_b)
        yx, _, _ = hybrid_mixer(modulate(x, sh1, sc1), p, rope, s_f, s_b)
        x = layer_norm(DEEPNORM_ALPHA * x + g1 * yx, ln1_g[l], ln1_b[l])
        x = layer_norm(DEEPNORM_ALPHA * x + g2 * sq_relu_mlp(modulate(x, sh2, sc2), p), ln2_g[l], ln2_b[l])
        if not last:
            ctx = layer_norm(DEEPNORM_ALPHA * ctx + cg1 * yc, ln1_g[l], ln1_b[l])
            ctx = layer_norm(DEEPNORM_ALPHA * ctx + cg2 * sq_relu_mlp(modulate(ctx, csh2, csc2), p),
                             ln2_g[l], ln2_b[l])
    return x
```

```python
import functools
import math

import jax
import jax.numpy as jnp
from jax import lax
from jax.experimental import pallas as pl
from jax.experimental.pallas import tpu as pltpu

F32 = jnp.float32
BF16 = jnp.bfloat16

D_MODEL = 2048
DEPTH = 4
GRID_W = 64
D_POOL = D_MODEL // 4
POOL_WINDOWS = (2, 4, 8, 16)
POOL_GROUP = D_POOL // len(POOL_WINDOWS)
D_HYENA = D_MODEL // 4
FILTER_EMB = 33
FILTER_BANDS = (FILTER_EMB - 1) // 2
FILTER_ORDER = 64
FILTER_DECAY_TARGET = 1e-2
FILTER_FAST_PCT = 0.3
FILTER_SLOW_PCT = 1.5
RET_HEAD_DIM = 256
D_RET = D_MODEL // 2
RET_HEADS = D_RET // RET_HEAD_DIM
RET_CHUNK = 128
ROPE_BASE = 10000.0
ROPE_PAIRS = RET_HEAD_DIM // 4
N_BRANCH = 3
D_FF = 4 * D_MODEL
LN_EPS = 1e-5
GN_EPS = 1e-6
DEEPNORM_ALPHA = (2 * DEPTH) ** 0.25
O_POOL = 0
O_HY = O_POOL + D_POOL
O_Q = O_HY + 3 * D_HYENA
O_K = O_Q + D_RET
O_V = O_K + D_RET
O_G = O_V + D_RET
O_GATE = O_G + D_RET
D_IN = O_GATE + N_BRANCH * D_MODEL

LANE = 128
FEAT_PAD = LANE
POOL_PAD = 16
ADA_ROWS = 8
MIB = 1024 * 1024


def _params(semantics, vmem_mib):
    return pltpu.CompilerParams(dimension_semantics=semantics, vmem_limit_bytes=vmem_mib * MIB)


def _silu(v):
    return v * jax.nn.sigmoid(v)


def _layer_norm(r, g, b):
    mu = jnp.mean(r, axis=-1, keepdims=True)
    d = r - mu
    var = jnp.mean(d * d, axis=-1, keepdims=True)
    return d * lax.rsqrt(var + LN_EPS) * g + b


def _ada_kernel(c_ref, w_ref, b_ref, o_ref):
    s = _silu(c_ref[...]).astype(BF16)
    o_ref[0] = jnp.dot(s, w_ref[0].astype(BF16), preferred_element_type=F32) + b_ref[0]


def _ada(cvec, w_ada, b_ada):
    tn = 1024
    n = w_ada.shape[2]
    return pl.pallas_call(
        _ada_kernel,
        grid=(DEPTH, n // tn),
        in_specs=[
            pl.BlockSpec((ADA_ROWS, D_MODEL), lambda l, j: (0, 0)),
            pl.BlockSpec((1, D_MODEL, tn), lambda l, j: (l, 0, j)),
            pl.BlockSpec((1, 1, tn), lambda l, j: (l, 0, j)),
        ],
        out_specs=pl.BlockSpec((1, ADA_ROWS, tn), lambda l, j: (l, 0, j)),
        out_shape=jax.ShapeDtypeStruct((DEPTH, ADA_ROWS, n), F32),
        compiler_params=_params(("parallel", "parallel"), 40),
        name="ada",
    )(cvec, w_ada, b_ada.reshape(DEPTH, 1, n))


def _modmm_kernel(x_ref, sh_ref, sc_ref, w_ref, b_ref, o_ref, xm_ref, *, sq_relu):
    @pl.when(pl.program_id(2) == 0)
    def _():
        xm_ref[...] = (x_ref[0] * (1.0 + sc_ref[0]) + sh_ref[0]).astype(BF16)

    y = jnp.dot(xm_ref[...], w_ref[...], preferred_element_type=F32) + b_ref[...]
    if sq_relu:
        y = jnp.square(jnp.maximum(y, 0.0))
    o_ref[0] = y.astype(o_ref.dtype)


def _modmm(x, mod, sh_blk, sc_blk, w, b, out_dtype, sq_relu, name):
    bsz, seq, _ = x.shape
    n = w.shape[1]
    tm = min(1024, seq)
    tn = 512
    return pl.pallas_call(
        functools.partial(_modmm_kernel, sq_relu=sq_relu),
        grid=(bsz, seq // tm, n // tn),
        in_specs=[
            pl.BlockSpec((1, tm, D_MODEL), lambda bi, i, j: (bi, i, 0)),
            pl.BlockSpec((1, 1, D_MODEL), lambda bi, i, j: (bi, 0, sh_blk)),
            pl.BlockSpec((1, 1, D_MODEL), lambda bi, i, j: (bi, 0, sc_blk)),
            pl.BlockSpec((D_MODEL, tn), lambda bi, i, j: (0, j)),
            pl.BlockSpec((1, tn), lambda bi, i, j: (0, j)),
        ],
        out_specs=pl.BlockSpec((1, tm, tn), lambda bi, i, j: (bi, i, j)),
        out_shape=jax.ShapeDtypeStruct((bsz, seq, n), out_dtype),
        scratch_shapes=[pltpu.VMEM((tm, D_MODEL), BF16)],
        compiler_params=_params(("parallel", "parallel", "arbitrary"), 48),
        name=name,
    )(x, mod, mod, w, b.reshape(1, n))


def _pool_kernel(z_ref, w_ref, s_ref, o_ref, pad_ref):
    seq = z_ref.shape[1]
    rows = seq + 2 * POOL_PAD
    zeros = jnp.zeros((POOL_PAD, POOL_GROUP), F32)
    pad_ref[0:POOL_PAD, :] = zeros
    pad_ref[POOL_PAD + seq:rows, :] = zeros
    t = lax.broadcasted_iota(jnp.int32, (seq, POOL_GROUP), 0)
    for g, win in enumerate(POOL_WINDOWS):
        cols = slice(g * POOL_GROUP, (g + 1) * POOL_GROUP)
        u = z_ref[0, :, cols]
        pad_ref[POOL_PAD:POOL_PAD + seq, :] = u
        w = pad_ref[...]
        w = pltpu.roll(w, 1, 0) + w
        width = 2
        while width < win:
            half = width // 2
            w = pltpu.roll(w, half, 0) + pltpu.roll(w, rows - half, 0)
            width *= 2
        half = win // 2
        count = jnp.minimum(t + half, seq) - jnp.maximum(t - half, 0)
        pooled = w[POOL_PAD:POOL_PAD + seq, :] / count.astype(F32) - u
        y = jnp.dot(pooled.astype(BF16), w_ref[g].astype(BF16), preferred_element_type=F32)
        o_ref[0, :, cols] = (y * s_ref[:, cols]).astype(BF16)


def _pool(z, pool_w, pool_scale):
    bsz, seq, _ = z.shape
    return pl.pallas_call(
        _pool_kernel,
        grid=(bsz,),
        in_specs=[
            pl.BlockSpec((1, seq, D_POOL), lambda bi: (bi, 0, O_POOL // D_POOL)),
            pl.BlockSpec((len(POOL_WINDOWS), POOL_GROUP, POOL_GROUP), lambda bi: (0, 0, 0)),
            pl.BlockSpec((1, D_POOL), lambda bi: (0, 0)),
        ],
        out_specs=pl.BlockSpec((1, seq, D_POOL), lambda bi: (bi, 0, 0)),
        out_shape=jax.ShapeDtypeStruct((bsz, seq, D_POOL), BF16),
        scratch_shapes=[pltpu.VMEM((seq + 2 * POOL_PAD, POOL_GROUP), F32)],
        compiler_params=_params(("parallel",), 48),
        name="pool",
    )(z, pool_w, pool_scale.reshape(1, D_POOL))


HY_CT = 256


def _conv3(u, w, b):
    seq = u.shape[0]
    t = lax.broadcasted_iota(jnp.int32, u.shape, 0)
    prev = jnp.where(t == 0, 0.0, pltpu.roll(u, 1, 0))
    nxt = jnp.where(t == seq - 1, 0.0, pltpu.roll(u, seq - 1, 0))
    return prev * w[0:1, :] + u * w[1:2, :] + nxt * w[2:3, :] + b


def _hy_pre_kernel(zv_ref, z0_ref, z1_ref, wv_ref, w0_ref, w1_ref, bv_ref, b0_ref, b1_ref,
                   uu_ref, x0_ref):
    v = _conv3(zv_ref[0], wv_ref[...], bv_ref[...])
    x1 = _conv3(z1_ref[0], w1_ref[...], b1_ref[...])
    uu_ref[0] = (v * x1).astype(BF16)
    x0_ref[0] = _conv3(z0_ref[0], w0_ref[...], b0_ref[...])


def _hy_pre(z, conv_w, conv_b):
    bsz, seq, _ = z.shape
    ct = HY_CT
    nct = D_HYENA // ct
    zoff = O_HY // ct
    zspec = [pl.BlockSpec((1, seq, ct), functools.partial(lambda bi, j, s: (bi, 0, zoff + s * nct + j), s=s))
             for s in range(3)]
    wspec = [pl.BlockSpec((3, ct), functools.partial(lambda bi, j, s: (0, s * nct + j), s=s))
             for s in range(3)]
    bspec = [pl.BlockSpec((1, ct), functools.partial(lambda bi, j, s: (0, s * nct + j), s=s))
             for s in range(3)]
    cb = conv_b.reshape(1, 3 * D_HYENA)
    return pl.pallas_call(
        _hy_pre_kernel,
        grid=(bsz, nct),
        in_specs=zspec + wspec + bspec,
        out_specs=[pl.BlockSpec((1, seq, ct), lambda bi, j: (bi, 0, j)),
                   pl.BlockSpec((1, seq, ct), lambda bi, j: (bi, 0, j))],
        out_shape=[jax.ShapeDtypeStruct((bsz, seq, D_HYENA), BF16),
                   jax.ShapeDtypeStruct((bsz, seq, D_HYENA), F32)],
        compiler_params=_params(("parallel", "parallel"), 48),
        name="hy_pre",
    )(z, z, z, conv_w, conv_w, conv_w, cb, cb, cb)


def _filt_kernel(zf_ref, w1_ref, b1_ref, f1_ref, w2_ref, b2_ref, f2_ref, w3_ref, b3_ref, f3_ref,
                 w4_ref, dl_ref, o_ref):
    tl = zf_ref.shape[0]
    zf = zf_ref[...]

    def dense(a, w_ref):
        return jnp.dot(a.astype(BF16), w_ref[...].astype(BF16), preferred_element_type=F32)

    hdn = jnp.sin(f1_ref[...] * (dense(zf, w1_ref) + b1_ref[...]))
    hdn = jnp.sin(f2_ref[...] * (dense(hdn, w2_ref) + b2_ref[...]))
    hdn = jnp.sin(f3_ref[...] * (dense(hdn, w3_ref) + b3_ref[...]))
    h = dense(hdn, w4_ref)
    decay = jnp.exp(-zf[:, 0:1] * jnp.abs(dl_ref[...]))
    row = lax.broadcasted_iota(jnp.int32, (tl, D_HYENA), 0) + pl.program_id(0) * tl
    o_ref[:, 0:D_HYENA] = (h[:, 0:D_HYENA] * decay).astype(BF16)
    o_ref[:, D_HYENA:] = jnp.where(row == 0, 0.0, h[:, D_HYENA:] * decay).astype(BF16)


def _filter_features(seq):
    t = jnp.linspace(0.0, 1.0, seq, dtype=F32)[:, None]
    w = 2.0 * math.pi * jnp.arange(seq, dtype=F32)[:, None] / seq
    f = jnp.linspace(1e-4, FILTER_BANDS - 1, FILTER_BANDS, dtype=F32)[None, :]
    z = jnp.concatenate([t, jnp.cos(f * w), -jnp.sin(f * w)], axis=-1)
    return jnp.pad(z, ((0, 0), (0, FEAT_PAD - FILTER_EMB)))


def _filter_deltas():
    max_decay = math.log(FILTER_DECAY_TARGET) / FILTER_FAST_PCT
    min_decay = math.log(FILTER_DECAY_TARGET) / FILTER_SLOW_PCT
    return jnp.linspace(min_decay, max_decay, D_HYENA, dtype=F32)[None, :]


def _filters(zfeat, deltas, p):
    seq = zfeat.shape[0]
    tl = min(256, seq)
    w1 = jnp.pad(p['filt_w1'], ((0, FEAT_PAD - FILTER_EMB), (0, 0)))
    row = lambda a: a.reshape(1, -1)
    full = lambda shape: pl.BlockSpec(shape, lambda i: (0,) * len(shape))
    vec = full((1, FILTER_ORDER))
    sq = full((FILTER_ORDER, FILTER_ORDER))
    return pl.pallas_call(
        _filt_kernel,
        grid=(seq // tl,),
        in_specs=[pl.BlockSpec((tl, FEAT_PAD), lambda i: (i, 0)),
                  full((FEAT_PAD, FILTER_ORDER)), vec, vec, sq, vec, vec, sq, vec, vec,
                  full((FILTER_ORDER, 2 * D_HYENA)), full((1, D_HYENA))],
        out_specs=pl.BlockSpec((tl, 2 * D_HYENA), lambda i: (i, 0)),
        out_shape=jax.ShapeDtypeStruct((seq, 2 * D_HYENA), BF16),
        compiler_params=_params(("parallel",), 32),
        name="hy_filter",
    )(zfeat, w1, row(p['filt_b1']), row(p['filt_f1']), p['filt_w2'], row(p['filt_b2']),
      row(p['filt_f2']), p['filt_w3'], row(p['filt_b3']), row(p['filt_f3']), p['filt_w4'], deltas)


def _dft_matrices(seq):
    n = 2 * seq
    f = jnp.arange(seq, dtype=jnp.int32)[:, None]
    t = jnp.arange(seq, dtype=jnp.int32)[None, :]
    ang = ((f * t) % n).astype(F32) * (2.0 * math.pi / n)
    cosm = jnp.cos(ang)
    nyq = jnp.where(t % 2 == 0, 1.0, -1.0).astype(F32)
    sinm = jnp.where(f == 0, nyq, jnp.sin(ang))
    fwd = jnp.stack([cosm, sinm]).astype(BF16)
    inv = jnp.stack([cosm.T, sinm.T]).astype(BF16)
    return fwd, inv


def _mm_kernel(a_ref, b_ref, o_ref):
    o_ref[0] = jnp.dot(a_ref[0], b_ref[...], preferred_element_type=F32)


def _filter_spectrum(fwd, hcat):
    _, seq, _ = fwd.shape
    n = hcat.shape[1]
    tm = min(512, seq)
    tn = 512
    return pl.pallas_call(
        _mm_kernel,
        grid=(2, seq // tm, n // tn),
        in_specs=[pl.BlockSpec((1, tm, seq), lambda h, i, j: (h, i, 0)),
                  pl.BlockSpec((seq, tn), lambda h, i, j: (0, j))],
        out_specs=pl.BlockSpec((1, tm, tn), lambda h, i, j: (h, i, j)),
        out_shape=jax.ShapeDtypeStruct((2, seq, n), F32),
        compiler_params=_params(("parallel", "parallel", "parallel"), 32),
        name="hy_filter_dft",
    )(fwd, hcat)


def _hy_fwd_kernel(f_ref, uu_ref, kf_ref, kb_ref, d_ref, y_ref, *, n_fft):
    tm = f_ref.shape[1]
    uu = uu_ref[0]
    a = jnp.dot(f_ref[0], uu, preferred_element_type=F32)
    b = jnp.dot(f_ref[1], uu, preferred_element_type=F32)
    row0 = (lax.broadcasted_iota(jnp.int32, a.shape, 0) + pl.program_id(2) * tm) == 0
    ka = kf_ref[0] + kb_ref[0] + d_ref[...]
    kb_sum = kf_ref[1] + kb_ref[1] + d_ref[...]
    kb_dif = kf_ref[1] - kb_ref[1]
    ya = jnp.where(row0, a * ka, a * ka - b * kb_dif)
    yb = jnp.where(row0, b * kb_sum, a * kb_dif + b * ka)
    wgt = jnp.where(row0, 1.0 / n_fft, 2.0 / n_fft)
    y_ref[0, 0] = (ya * wgt).astype(BF16)
    y_ref[0, 1] = (yb * wgt).astype(BF16)


def _hy_fwd(fwd, uu, kspec, hyena_d):
    bsz, seq, _ = uu.shape
    ct = HY_CT
    nct = D_HYENA // ct
    tm = min(512, seq)
    return pl.pallas_call(
        functools.partial(_hy_fwd_kernel, n_fft=2 * seq),
        grid=(bsz, nct, seq // tm),
        in_specs=[pl.BlockSpec((2, tm, seq), lambda bi, j, i: (0, i, 0)),
                  pl.BlockSpec((1, seq, ct), lambda bi, j, i: (bi, 0, j)),
                  pl.BlockSpec((2, tm, ct), lambda bi, j, i: (0, i, j)),
                  pl.BlockSpec((2, tm, ct), lambda bi, j, i: (0, i, nct + j)),
                  pl.BlockSpec((1, ct), lambda bi, j, i: (0, j))],
        out_specs=pl.BlockSpec((1, 2, tm, ct), lambda bi, j, i: (bi, 0, i, j)),
        out_shape=jax.ShapeDtypeStruct((bsz, 2, seq, D_HYENA), BF16),
        compiler_params=_params(("parallel", "parallel", "arbitrary"), 48),
        name="hy_fwd_dft",
    )(fwd, uu, kspec, kspec, hyena_d.reshape(1, D_HYENA))


def _hy_inv_kernel(ft_ref, y_ref, x0_ref, o_ref):
    y = (jnp.dot(ft_ref[0], y_ref[0, 0], preferred_element_type=F32)
         + jnp.dot(ft_ref[1], y_ref[0, 1], preferred_element_type=F32))
    o_ref[0] = (y * x0_ref[0]).astype(BF16)


def _hy_inv(inv, yspec, x0):
    bsz, _, seq, _ = yspec.shape
    ct = HY_CT
    nct = D_HYENA // ct
    tm = min(512, seq)
    return pl.pallas_call(
        _hy_inv_kernel,
        grid=(bsz, nct, seq // tm),
        in_specs=[pl.BlockSpec((2, tm, seq), lambda bi, j, i: (0, i, 0)),
                  pl.BlockSpec((1, 2, seq, ct), lambda bi, j, i: (bi, 0, 0, j)),
                  pl.BlockSpec((1, tm, ct), lambda bi, j, i: (bi, i, j))],
        out_specs=pl.BlockSpec((1, tm, ct), lambda bi, j, i: (bi, i, j)),
        out_shape=jax.ShapeDtypeStruct((bsz, seq, D_HYENA), BF16),
        compiler_params=_params(("parallel", "parallel", "arbitrary"), 48),
        name="hy_inv_dft",
    )(inv, yspec, x0)


def _swap_halves(x):
    return jnp.concatenate([pltpu.roll(x[:, :LANE], LANE // 2, 1),
                            pltpu.roll(x[:, LANE:], LANE // 2, 1)], axis=1)


def _ret_kernel(*refs, use_rope, need_out):
    refs = list(refs)
    dec_ref, q_ref, k_ref, v_ref, g_ref = refs[:5]
    refs = refs[5:]
    if use_rope:
        cos_ref, sin_ref = refs[:2]
        refs = refs[2:]
    sf0_ref, sb0_ref = refs[:2]
    refs = refs[2:]
    if need_out:
        y_ref = refs[0]
        refs = refs[1:]
    sfo_ref, sbo_ref, qs, ks, vs, acc, st = refs

    seq = k_ref.shape[1]
    csz = RET_CHUNK
    n_chunks = seq // csz
    hd = RET_HEAD_DIM

    lg = jnp.log1p(-jnp.exp(dec_ref[0]))
    lgf = lg[0:1, :]
    lgb = lg[1:2, :]
    pos = lax.broadcasted_iota(jnp.int32, (csz, hd), 0).astype(F32)
    qdec_f = jnp.exp((pos + 1.0) * lgf)
    kdec_f = jnp.exp((csz - 1.0 - pos) * lgf)
    qdec_b = jnp.exp((csz - pos) * lgb)
    kdec_b = jnp.exp(pos * lgb)
    cdec_f = jnp.exp(csz * lgf)
    cdec_b = jnp.exp(csz * lgb)
    ii = lax.broadcasted_iota(jnp.int32, (csz, csz), 0)
    jj = lax.broadcasted_iota(jnp.int32, (csz, csz), 1)
    rel = (ii - jj).astype(F32)
    mask = (jnp.where(rel >= 0, jnp.exp(jnp.maximum(rel, 0.0) * lgf[:, :csz]), 0.0)
            + jnp.where(rel <= 0, jnp.exp(jnp.maximum(-rel, 0.0) * lgb[:, :csz]), 0.0))

    for c in range(n_chunks):
        rows = slice(c * csz, (c + 1) * csz)
        kc = k_ref[0, rows, :] * (hd ** -0.5)
        if use_rope:
            kc = kc * cos_ref[rows, :] + _swap_halves(kc) * sin_ref[rows, :]
        ks[rows, :] = kc
        vs[rows, :] = v_ref[0, rows, :].astype(BF16)
        if need_out:
            qc = q_ref[0, rows, :]
            if use_rope:
                qc = qc * cos_ref[rows, :] + _swap_halves(qc) * sin_ref[rows, :]
            qs[rows, :] = qc

    st[0] = sf0_ref[0, 0]
    st[1] = sb0_ref[0, 0]
    tn_dims = (((0,), (0,)), ((), ()))
    nt_dims = (((1,), (1,)), ((), ()))

    def fwd_body(c, carry):
        rows = pl.ds(pl.multiple_of(c * csz, csz), csz)
        kc = ks[rows, :]
        vc = vs[rows, :]
        if need_out:
            qc = qs[rows, :]
            s = lax.dot_general(qc.astype(BF16), kc.astype(BF16), nt_dims, preferred_element_type=F32)
            o = jnp.dot((s * mask).astype(BF16), vc, preferred_element_type=F32)
            o = o + jnp.dot((qc * qdec_f).astype(BF16), st[0].astype(BF16), preferred_element_type=F32)
            acc[rows, :] = o
        st[0] = st[0] * cdec_f + lax.dot_general((kc * kdec_f).astype(BF16), vc, tn_dims,
                                                 preferred_element_type=F32)
        return carry

    lax.fori_loop(0, n_chunks, fwd_body, 0)

    def bwd_body(i, carry):
        c = n_chunks - 1 - i
        rows = pl.ds(pl.multiple_of(c * csz, csz), csz)
        kc = ks[rows, :]
        vc = vs[rows, :]
        if need_out:
            qc = qs[rows, :]
            o = acc[rows, :] + jnp.dot((qc * qdec_b).astype(BF16), st[1].astype(BF16),
                                       preferred_element_type=F32)
            mu = jnp.mean(o, axis=-1, keepdims=True)
            d = o - mu
            var = jnp.mean(d * d, axis=-1, keepdims=True)
            y_ref[0, rows, :] = (_silu(g_ref[0, rows, :]) * (d * lax.rsqrt(var + GN_EPS))).astype(BF16)
        st[1] = st[1] * cdec_b + lax.dot_general((kc * kdec_b).astype(BF16), vc, tn_dims,
                                                 preferred_element_type=F32)
        return carry

    lax.fori_loop(0, n_chunks, bwd_body, 0)
    sfo_ref[0, 0] = st[0]
    sbo_ref[0, 0] = st[1]


def _retention(z, kv_only, ret_decay, rope, s_f, s_b):
    bsz, seq, _ = z.shape
    hd = RET_HEAD_DIM
    need_out = not kv_only
    use_rope = rope is not None
    if kv_only:
        oq, ok, ov, og = 0, 0, D_RET // hd, 0
    else:
        oq, ok, ov, og = O_Q // hd, O_K // hd, O_V // hd, O_G // hd
    dec = jnp.broadcast_to(ret_decay.T[:, :, None], (RET_HEADS, 2, hd))
    col = lambda off: pl.BlockSpec((1, seq, hd), lambda bi, h: (bi, 0, off + h))
    state = pl.BlockSpec((1, 1, hd, hd), lambda bi, h: (bi, h, 0, 0))
    in_specs = [pl.BlockSpec((1, 2, hd), lambda bi, h: (h, 0, 0)), col(oq), col(ok), col(ov), col(og)]
    args = [dec, z, z, z, z]
    if use_rope:
        in_specs += [pl.BlockSpec((seq, hd), lambda bi, h: (0, 0))] * 2
        args += list(rope)
    in_specs += [state, state]
    args += [s_f, s_b]
    out_specs = [state, state]
    out_shape = [jax.ShapeDtypeStruct((bsz, RET_HEADS, hd, hd), F32)] * 2
    if need_out:
        out_specs = [pl.BlockSpec((1, seq, hd), lambda bi, h: (bi, 0, h))] + out_specs
        out_shape = [jax.ShapeDtypeStruct((bsz, seq, D_RET), BF16)] + out_shape
    res = pl.pallas_call(
        functools.partial(_ret_kernel, use_rope=use_rope, need_out=need_out),
        grid=(bsz, RET_HEADS),
        in_specs=in_specs,
        out_specs=out_specs,
        out_shape=out_shape,
        scratch_shapes=[pltpu.VMEM((seq, hd), F32), pltpu.VMEM((seq, hd), F32),
                        pltpu.VMEM((seq, hd), BF16), pltpu.VMEM((seq, hd), F32),
                        pltpu.VMEM((2, hd, hd), F32)],
        compiler_params=_params(("parallel", "parallel"), 56),
        name="retention" if need_out else "retention_state",
    )(*args)
    if need_out:
        return res[0], res[1], res[2]
    return None, res[0], res[1]


def _rope_tables(seq):
    rows = seq // GRID_W
    row = jnp.repeat(jnp.arange(rows, dtype=F32), GRID_W)
    colp = jnp.tile(jnp.arange(GRID_W, dtype=F32), rows)
    inv = ROPE_BASE ** (-jnp.arange(ROPE_PAIRS, dtype=F32) / ROPE_PAIRS)
    ang_r = row[:, None] * inv[None, :]
    ang_c = colp[:, None] * inv[None, :]
    cr, sr, cc, sc = jnp.cos(ang_r), jnp.sin(ang_r), jnp.cos(ang_c), jnp.sin(ang_c)
    return (jnp.concatenate([cr, cr, cc, cc], axis=-1),
            jnp.concatenate([-sr, sr, -sc, sc], axis=-1))


def _merge_kernel(ya_ref, yb_ref, yc_ref, ga_ref, gb_ref, gc_ref, pa_ref, pb_ref, pc_ref, o_ref):
    m = jax.nn.sigmoid(ga_ref[0]) * jnp.dot(ya_ref[0], pa_ref[...], preferred_element_type=F32)
    m = m + jax.nn.sigmoid(gb_ref[0]) * jnp.dot(yb_ref[0], pb_ref[...], preferred_element_type=F32)
    m = m + jax.nn.sigmoid(gc_ref[0]) * jnp.dot(yc_ref[0], pc_ref[...], preferred_element_type=F32)
    o_ref[0] = m.astype(BF16)


def _merge(ya, yb, yc, z, p_a, p_b, p_c):
    bsz, seq, _ = ya.shape
    tm = min(512, seq)
    tn = 512
    goff = O_GATE // tn
    gstep = D_MODEL // tn
    act = lambda width: pl.BlockSpec((1, tm, width), lambda bi, i, j: (bi, i, 0))
    gate = lambda br: pl.BlockSpec((1, tm, tn), lambda bi, i, j: (bi, i, goff + br * gstep + j))
    wgt = lambda rows: pl.BlockSpec((rows, tn), lambda bi, i, j: (0, j))
    return pl.pallas_call(
        _merge_kernel,
        grid=(bsz, seq // tm, D_MODEL // tn),
        in_specs=[act(D_POOL), act(D_HYENA), act(D_RET), gate(0), gate(1), gate(2),
                  wgt(D_POOL), wgt(D_HYENA), wgt(D_RET)],
        out_specs=pl.BlockSpec((1, tm, tn), lambda bi, i, j: (bi, i, j)),
        out_shape=jax.ShapeDtypeStruct((bsz, seq, D_MODEL), BF16),
        compiler_params=_params(("parallel", "parallel", "arbitrary"), 48),
        name="merge",
    )(ya, yb, yc, z, z, z, p_a, p_b, p_c)


def _res_ln_kernel(a_ref, w_ref, b_ref, x_ref, gt_ref, g_ref, be_ref, o_ref, acc_ref):
    k = pl.program_id(2)

    @pl.when(k == 0)
    def _():
        acc_ref[...] = jnp.zeros_like(acc_ref)

    acc_ref[...] += jnp.dot(a_ref[0], w_ref[...], preferred_element_type=F32)

    @pl.when(k == pl.num_programs(2) - 1)
    def _():
        r = DEEPNORM_ALPHA * x_ref[0] + gt_ref[0] * (acc_ref[...] + b_ref[...])
        o_ref[0] = _layer_norm(r, g_ref[...], be_ref[...])


def _res_ln(a, w, b, x, mod, gate_blk, ln_g, ln_b, name):
    bsz, seq, kdim = a.shape
    tm = min(512, seq)
    tk = min(1024, kdim)
    row = lambda v: v.reshape(1, D_MODEL)
    vec = pl.BlockSpec((1, D_MODEL), lambda bi, i, k: (0, 0))
    return pl.pallas_call(
        _res_ln_kernel,
        grid=(bsz, seq // tm, kdim // tk),
        in_specs=[pl.BlockSpec((1, tm, tk), lambda bi, i, k: (bi, i, k)),
                  pl.BlockSpec((tk, D_MODEL), lambda bi, i, k: (k, 0)),
                  vec,
                  pl.BlockSpec((1, tm, D_MODEL), lambda bi, i, k: (bi, i, 0)),
                  pl.BlockSpec((1, 1, D_MODEL), lambda bi, i, k: (bi, 0, gate_blk)),
                  vec, vec],
        out_specs=pl.BlockSpec((1, tm, D_MODEL), lambda bi, i, k: (bi, i, 0)),
        out_shape=jax.ShapeDtypeStruct((bsz, seq, D_MODEL), F32),
        scratch_shapes=[pltpu.VMEM((tm, D_MODEL), F32)],
        compiler_params=_params(("parallel", "parallel", "arbitrary"), 48),
        name=name,
    )(a, w, row(b), x, mod, row(ln_g), row(ln_b))


def _mixer(h, mod, p, tables, rope, s_f, s_b):
    fwd, inv, zfeat, deltas = tables
    z = _modmm(h, mod, 0, 1, p['w_in'], p['b_in'], F32, False, "in_proj")
    ya = _pool(z, p['pool_w'], p['pool_scale'])
    uu, x0 = _hy_pre(z, p['conv_w'], p['conv_b'])
    kspec = _filter_spectrum(fwd, _filters(zfeat, deltas, p))
    yb = _hy_inv(inv, _hy_fwd(fwd, uu, kspec, p['hyena_d']), x0)
    yc, sfo, sbo = _retention(z, False, p['ret_decay'], rope, s_f, s_b)
    return _merge(ya, yb, yc, z, p['p_a'], p['p_b'], p['p_c']), sfo, sbo


def _block_tail(merged, x, mod, p):
    x = _res_ln(merged, p['w_o'], p['b_o'], x, mod, 2, p['ln1_g'], p['ln1_b'], "out_proj_ln1")
    hid = _modmm(x, mod, 3, 4, p['w_mlp1'], p['b_mlp1'], BF16, True, "mlp_up")
    return _res_ln(hid, p['w_mlp2'], p['b_mlp2'], x, mod, 5, p['ln2_g'], p['ln2_b'], "mlp_down_ln2")


def kernel(x, c, ctx, c_ctx, w_ada, b_ada, w_in, b_in, conv_w, conv_b, pool_w, pool_scale, filt_w1, filt_b1, filt_f1, filt_w2, filt_b2, filt_f2, filt_w3, filt_b3, filt_f3, filt_w4, hyena_d, ret_decay, p_a, p_b, p_c, w_o, b_o, ln1_g, ln1_b, w_mlp1, b_mlp1, w_mlp2, b_mlp2, ln2_g, ln2_b):
    bsz, seq, _ = x.shape
    ctx_len = ctx.shape[1]
    assert x.shape == (bsz, seq, D_MODEL) and ctx.shape == (bsz, ctx_len, D_MODEL)
    assert seq % RET_CHUNK == 0 and ctx_len % RET_CHUNK == 0 and seq % GRID_W == 0
    assert bsz + 1 <= ADA_ROWS

    cvec = jnp.concatenate([c, c_ctx[None, :], jnp.zeros((ADA_ROWS - bsz - 1, D_MODEL), F32)], axis=0)
    mod_all = _ada(cvec, w_ada, b_ada)

    deltas = _filter_deltas()
    tables_x = _dft_matrices(seq) + (_filter_features(seq), deltas)
    tables_c = _dft_matrices(ctx_len) + (_filter_features(ctx_len), deltas)
    rope = _rope_tables(seq)
    zero_state = jnp.zeros((bsz, RET_HEADS, RET_HEAD_DIM, RET_HEAD_DIM), F32)

    for l in range(DEPTH):
        last = l == DEPTH - 1
        p = {
            'w_in': w_in[l].astype(BF16), 'b_in': b_in[l], 'conv_w': conv_w[l], 'conv_b': conv_b[l],
            'pool_w': pool_w[l], 'pool_scale': pool_scale[l],
            'filt_w1': filt_w1[l], 'filt_b1': filt_b1[l], 'filt_f1': filt_f1[l],
            'filt_w2': filt_w2[l], 'filt_b2': filt_b2[l], 'filt_f2': filt_f2[l],
            'filt_w3': filt_w3[l], 'filt_b3': filt_b3[l], 'filt_f3': filt_f3[l],
            'filt_w4': filt_w4[l], 'hyena_d': hyena_d[l], 'ret_decay': ret_decay[l],
            'p_a': p_a[l].astype(BF16), 'p_b': p_b[l].astype(BF16), 'p_c': p_c[l].astype(BF16),
            'w_o': w_o[l].astype(BF16), 'b_o': b_o[l], 'ln1_g': ln1_g[l], 'ln1_b': ln1_b[l],
            'w_mlp1': w_mlp1[l].astype(BF16), 'b_mlp1': b_mlp1[l],
            'w_mlp2': w_mlp2[l].astype(BF16), 'b_mlp2': b_mlp2[l],
            'ln2_g': ln2_g[l], 'ln2_b': ln2_b[l],
        }
        mod_x = mod_all[l, :bsz][:, None, :]
        mod_c = jnp.broadcast_to(mod_all[l, bsz][None, None, :], (bsz, 1, 6 * D_MODEL))

        if not last:
            merged_c, s_f, s_b = _mixer(ctx, mod_c, p, tables_c, None, zero_state, zero_state)
        else:
            zkv = _modmm(ctx, mod_c, 0, 1, p['w_in'][:, O_K:O_G], p['b_in'][O_K:O_G], F32, False,
                         "in_proj_kv")
            _, s_f, s_b = _retention(zkv, True, p['ret_decay'], None, zero_state, zero_state)
        merged_x, _, _ = _mixer(x, mod_x, p, tables_x, rope, s_f, s_b)
        x = _block_tail(merged_x, x, mod_x, p)
        if not last:
            ctx = _block_tail(merged_c, ctx, mod_c, p)
    return x
```

```python
import functools
import math

import jax
import jax.numpy as jnp
from jax import lax
from jax.experimental import pallas as pl
from jax.experimental.pallas import tpu as pltpu

F32 = jnp.float32
BF16 = jnp.bfloat16

D_MODEL = 2048
DEPTH = 4
GRID_W = 64
D_POOL = D_MODEL // 4
POOL_WINDOWS = (2, 4, 8, 16)
POOL_GROUP = D_POOL // len(POOL_WINDOWS)
D_HYENA = D_MODEL // 4
FILTER_EMB = 33
FILTER_BANDS = (FILTER_EMB - 1) // 2
FILTER_ORDER = 64
FILTER_DECAY_TARGET = 1e-2
FILTER_FAST_PCT = 0.3
FILTER_SLOW_PCT = 1.5
RET_HEAD_DIM = 256
D_RET = D_MODEL // 2
RET_HEADS = D_RET // RET_HEAD_DIM
RET_CHUNK = 128
ROPE_BASE = 10000.0
ROPE_PAIRS = RET_HEAD_DIM // 4
N_BRANCH = 3
D_FF = 4 * D_MODEL
LN_EPS = 1e-5
GN_EPS = 1e-6
DEEPNORM_ALPHA = (2 * DEPTH) ** 0.25
O_POOL = 0
O_HY = O_POOL + D_POOL
O_Q = O_HY + 3 * D_HYENA
O_K = O_Q + D_RET
O_V = O_K + D_RET
O_G = O_V + D_RET
O_GATE = O_G + D_RET
D_IN = O_GATE + N_BRANCH * D_MODEL

LANE = 128
FEAT_PAD = LANE
POOL_PAD = 16
ADA_ROWS = 8
MIB = 1024 * 1024

TM_WS = 1024
TN_WS = 1024
TM_ROW = 512
HY_CT = 256
RET_HPS = 2


def _params(semantics, vmem_mib):
    return pltpu.CompilerParams(dimension_semantics=semantics, vmem_limit_bytes=vmem_mib * MIB)


def _silu(v):
    return v * jax.nn.sigmoid(v)


def _layer_norm(r, g, b):
    mu = jnp.mean(r, axis=-1, keepdims=True)
    d = r - mu
    var = jnp.mean(d * d, axis=-1, keepdims=True)
    return d * lax.rsqrt(var + LN_EPS) * g + b


def _skip_ref(kern, idx):
    def wrapped(*refs):
        return kern(*refs[:idx], *refs[idx + 1:])
    return wrapped


def _seg_call(kern, *, prev, n_in, **kw):
    in_specs = list(kw.pop("in_specs"))
    args = list(kw.pop("args"))
    if prev is None:
        return pl.pallas_call(kern, in_specs=in_specs, **kw)(*args)
    in_specs.append(pl.BlockSpec(memory_space=pl.ANY))
    args.append(prev)
    return pl.pallas_call(_skip_ref(kern, n_in), in_specs=in_specs,
                          input_output_aliases={n_in: 0}, **kw)(*args)


def _mod_index(layer, per_batch_tiles, bsz):
    return lambda i: layer * ADA_ROWS + jnp.minimum(i // per_batch_tiles, bsz)


def _ada_kernel(c_ref, w_ref, b_ref, o_ref):
    s = _silu(c_ref[...]).astype(BF16)
    o_ref[0] = jnp.dot(s, w_ref[0].astype(BF16), preferred_element_type=F32) + b_ref[0]


def _ada(cvec, w_ada, b_ada):
    tn = 1024
    n = w_ada.shape[2]
    return pl.pallas_call(
        _ada_kernel,
        grid=(DEPTH, n // tn),
        in_specs=[
            pl.BlockSpec((ADA_ROWS, D_MODEL), lambda l, j: (0, 0)),
            pl.BlockSpec((1, D_MODEL, tn), lambda l, j: (l, 0, j)),
            pl.BlockSpec((1, 1, tn), lambda l, j: (l, 0, j)),
        ],
        out_specs=pl.BlockSpec((1, ADA_ROWS, tn), lambda l, j: (l, 0, j)),
        out_shape=jax.ShapeDtypeStruct((DEPTH, ADA_ROWS, n), F32),
        compiler_params=_params(("parallel", "parallel"), 40),
        name="ada",
    )(cvec, w_ada, b_ada.reshape(DEPTH, 1, n))


def _mod0_kernel(x_ref, c_ref, sh_ref, sc_ref, xo_ref, xm_ref, *, n_lat_tiles):
    def emit(v):
        xo_ref[...] = v
        xm_ref[...] = (v * (1.0 + sc_ref[0]) + sh_ref[0]).astype(BF16)

    i = pl.program_id(0)

    @pl.when(i < n_lat_tiles)
    def _():
        emit(x_ref[...])

    @pl.when(i >= n_lat_tiles)
    def _():
        emit(c_ref[...])


def _mod0(x2d, c2d, mod, seq, bsz):
    n_lat, n_ctx = x2d.shape[0], c2d.shape[0]
    tm = TM_ROW
    nl = n_lat // tm
    midx = _mod_index(0, seq // tm, bsz)
    return pl.pallas_call(
        functools.partial(_mod0_kernel, n_lat_tiles=nl),
        grid=((n_lat + n_ctx) // tm,),
        in_specs=[pl.BlockSpec((tm, D_MODEL), lambda i: (jnp.minimum(i, nl - 1), 0)),
                  pl.BlockSpec((tm, D_MODEL), lambda i: (jnp.maximum(i - nl, 0), 0)),
                  pl.BlockSpec((1, 1, D_MODEL), lambda i: (midx(i), 0, 0)),
                  pl.BlockSpec((1, 1, D_MODEL), lambda i: (midx(i), 0, 1))],
        out_specs=[pl.BlockSpec((tm, D_MODEL), lambda i: (i, 0)),
                   pl.BlockSpec((tm, D_MODEL), lambda i: (i, 0))],
        out_shape=[jax.ShapeDtypeStruct((n_lat + n_ctx, D_MODEL), F32),
                   jax.ShapeDtypeStruct((n_lat + n_ctx, D_MODEL), BF16)],
        compiler_params=_params(("parallel",), 40),
        name="assemble_modulate",
    )(x2d, c2d, mod, mod)


def _wsmm_kernel(x_ref, w_ref, b_ref, o_ref, wb_ref, *, sq_relu):
    @pl.when(pl.program_id(1) == 0)
    def _():
        wb_ref[...] = w_ref[0].astype(BF16)

    y = jnp.dot(x_ref[...], wb_ref[...], preferred_element_type=F32) + b_ref[0]
    if sq_relu:
        y = jnp.square(jnp.maximum(y, 0.0))
    o_ref[...] = y.astype(BF16)


def _wsmm(xm, w_all, b_all, layer, m_rows, sq_relu, name):
    kdim, n = w_all.shape[1], w_all.shape[2]
    tm, tn = TM_WS, TN_WS
    return pl.pallas_call(
        functools.partial(_wsmm_kernel, sq_relu=sq_relu),
        grid=(n // tn, m_rows // tm),
        in_specs=[pl.BlockSpec((tm, kdim), lambda j, i: (i, 0)),
                  pl.BlockSpec((1, kdim, tn), lambda j, i: (layer, 0, j)),
                  pl.BlockSpec((1, 1, tn), lambda j, i: (layer, 0, j))],
        out_specs=pl.BlockSpec((tm, tn), lambda j, i: (i, j)),
        out_shape=jax.ShapeDtypeStruct((xm.shape[0], n), BF16),
        scratch_shapes=[pltpu.VMEM((kdim, tn), BF16)],
        compiler_params=_params(("parallel", "arbitrary"), 48),
        name=name,
    )(xm, w_all, b_all.reshape(DEPTH, 1, n))


def _pool_kernel(z_ref, w_ref, s_ref, o_ref, pad_ref):
    seq = z_ref.shape[0]
    rows = seq + 2 * POOL_PAD
    zeros = jnp.zeros((POOL_PAD, POOL_GROUP), F32)
    pad_ref[0:POOL_PAD, :] = zeros
    pad_ref[POOL_PAD + seq:rows, :] = zeros
    t = lax.broadcasted_iota(jnp.int32, (seq, POOL_GROUP), 0)
    for g, win in enumerate(POOL_WINDOWS):
        cols = slice(g * POOL_GROUP, (g + 1) * POOL_GROUP)
        u = z_ref[:, cols].astype(F32)
        pad_ref[POOL_PAD:POOL_PAD + seq, :] = u
        w = pad_ref[...]
        w = pltpu.roll(w, 1, 0) + w
        width = 2
        while width < win:
            half = width // 2
            w = pltpu.roll(w, half, 0) + pltpu.roll(w, rows - half, 0)
            width *= 2
        half = win // 2
        count = jnp.minimum(t + half, seq) - jnp.maximum(t - half, 0)
        pooled = w[POOL_PAD:POOL_PAD + seq, :] / count.astype(F32) - u
        y = jnp.dot(pooled.astype(BF16), w_ref[0, g].astype(BF16), preferred_element_type=F32)
        o_ref[:, cols] = (y * s_ref[0, :, cols]).astype(BF16)


def _pool(z, seg, layer, pool_w, pool_scale, prev):
    bsz, seq, rb0 = seg
    ng = len(POOL_WINDOWS)
    return _seg_call(
        _pool_kernel, prev=prev, n_in=3,
        grid=(bsz,),
        in_specs=[pl.BlockSpec((seq, D_POOL), lambda bi: (rb0 + bi, O_POOL // D_POOL)),
                  pl.BlockSpec((1, ng, POOL_GROUP, POOL_GROUP), lambda bi: (layer, 0, 0, 0)),
                  pl.BlockSpec((1, 1, D_POOL), lambda bi: (layer, 0, 0))],
        out_specs=pl.BlockSpec((seq, D_POOL), lambda bi: (rb0 + bi, 0)),
        out_shape=jax.ShapeDtypeStruct((z.shape[0], D_POOL), BF16),
        scratch_shapes=[pltpu.VMEM((seq + 2 * POOL_PAD, POOL_GROUP), F32)],
        compiler_params=_params(("parallel",), 48),
        name="pool",
        args=[z, pool_w, pool_scale.reshape(DEPTH, 1, D_POOL)],
    )


def _conv3(u, w, b):
    seq = u.shape[0]
    t = lax.broadcasted_iota(jnp.int32, u.shape, 0)
    prev = jnp.where(t == 0, 0.0, pltpu.roll(u, 1, 0))
    nxt = jnp.where(t == seq - 1, 0.0, pltpu.roll(u, seq - 1, 0))
    return prev * w[0:1, :] + u * w[1:2, :] + nxt * w[2:3, :] + b


def _hy_pre_kernel(zv_ref, z0_ref, z1_ref, wv_ref, w0_ref, w1_ref, bv_ref, b0_ref, b1_ref,
                   uu_ref, x0_ref):
    v = _conv3(zv_ref[...].astype(F32), wv_ref[0], bv_ref[0])
    x1 = _conv3(z1_ref[...].astype(F32), w1_ref[0], b1_ref[0])
    uu_ref[0] = (v * x1).astype(BF16)
    x0_ref[0] = _conv3(z0_ref[...].astype(F32), w0_ref[0], b0_ref[0]).astype(BF16)


def _hy_pre(z, seg, layer, conv_w, conv_b):
    bsz, seq, rb0 = seg
    ct = HY_CT
    nct = D_HYENA // ct
    zoff = O_HY // ct
    zspec = [pl.BlockSpec((seq, ct), functools.partial(lambda bi, j, s: (rb0 + bi, zoff + s * nct + j), s=s))
             for s in range(3)]
    wspec = [pl.BlockSpec((1, 3, ct), functools.partial(lambda bi, j, s: (layer, 0, s * nct + j), s=s))
             for s in range(3)]
    bspec = [pl.BlockSpec((1, 1, ct), functools.partial(lambda bi, j, s: (layer, 0, s * nct + j), s=s))
             for s in range(3)]
    cb = conv_b.reshape(DEPTH, 1, 3 * D_HYENA)
    out = pl.BlockSpec((1, seq, ct), lambda bi, j: (bi, 0, j))
    return pl.pallas_call(
        _hy_pre_kernel,
        grid=(bsz, nct),
        in_specs=zspec + wspec + bspec,
        out_specs=[out, out],
        out_shape=[jax.ShapeDtypeStruct((bsz, seq, D_HYENA), BF16)] * 2,
        compiler_params=_params(("parallel", "parallel"), 48),
        name="hy_pre",
    )(z, z, z, conv_w, conv_w, conv_w, cb, cb, cb)


def _filt_kernel(zf_ref, w1_ref, b1_ref, f1_ref, w2_ref, b2_ref, f2_ref, w3_ref, b3_ref, f3_ref,
                 w4_ref, dl_ref, o_ref):
    tl = zf_ref.shape[0]
    zf = zf_ref[...]

    def dense(a, w_ref):
        return jnp.dot(a.astype(BF16), w_ref[0].astype(BF16), preferred_element_type=F32)

    hdn = jnp.sin(f1_ref[0] * (dense(zf, w1_ref) + b1_ref[0]))
    hdn = jnp.sin(f2_ref[0] * (dense(hdn, w2_ref) + b2_ref[0]))
    hdn = jnp.sin(f3_ref[0] * (dense(hdn, w3_ref) + b3_ref[0]))
    h = dense(hdn, w4_ref)
    decay = jnp.exp(-zf[:, 0:1] * jnp.abs(dl_ref[...]))
    row = lax.broadcasted_iota(jnp.int32, (tl, D_HYENA), 0) + pl.program_id(0) * tl
    o_ref[:, 0:D_HYENA] = (h[:, 0:D_HYENA] * decay).astype(BF16)
    o_ref[:, D_HYENA:] = jnp.where(row == 0, 0.0, h[:, D_HYENA:] * decay).astype(BF16)


def _filter_features(seq):
    t = jnp.linspace(0.0, 1.0, seq, dtype=F32)[:, None]
    w = 2.0 * math.pi * jnp.arange(seq, dtype=F32)[:, None] / seq
    f = jnp.linspace(1e-4, FILTER_BANDS - 1, FILTER_BANDS, dtype=F32)[None, :]
    z = jnp.concatenate([t, jnp.cos(f * w), -jnp.sin(f * w)], axis=-1)
    return jnp.pad(z, ((0, 0), (0, FEAT_PAD - FILTER_EMB)))


def _filter_deltas():
    max_decay = math.log(FILTER_DECAY_TARGET) / FILTER_FAST_PCT
    min_decay = math.log(FILTER_DECAY_TARGET) / FILTER_SLOW_PCT
    return jnp.linspace(min_decay, max_decay, D_HYENA, dtype=F32)[None, :]


def _filters(zfeat, deltas, layer, fp):
    seq = zfeat.shape[0]
    tl = min(256, seq)
    lsel = lambda shape: pl.BlockSpec((1,) + shape, lambda i: (layer,) + (0,) * len(shape))
    vec = lsel((1, FILTER_ORDER))
    sq = lsel((FILTER_ORDER, FILTER_ORDER))
    return pl.pallas_call(
        _filt_kernel,
        grid=(seq // tl,),
        in_specs=[pl.BlockSpec((tl, FEAT_PAD), lambda i: (i, 0)),
                  lsel((FEAT_PAD, FILTER_ORDER)), vec, vec, sq, vec, vec, sq, vec, vec,
                  lsel((FILTER_ORDER, 2 * D_HYENA)),
                  pl.BlockSpec((1, D_HYENA), lambda i: (0, 0))],
        out_specs=pl.BlockSpec((tl, 2 * D_HYENA), lambda i: (i, 0)),
        out_shape=jax.ShapeDtypeStruct((seq, 2 * D_HYENA), BF16),
        compiler_params=_params(("parallel",), 32),
        name="hy_filter",
    )(zfeat, fp['w1'], fp['b1'], fp['f1'], fp['w2'], fp['b2'], fp['f2'], fp['w3'], fp['b3'], fp['f3'],
      fp['w4'], deltas)


def _dft_matrices(seq):
    n = 2 * seq
    f = jnp.arange(seq, dtype=jnp.int32)[:, None]
    t = jnp.arange(seq, dtype=jnp.int32)[None, :]
    ang = ((f * t) % n).astype(F32) * (2.0 * math.pi / n)
    cosm = jnp.cos(ang)
    nyq = jnp.where(t % 2 == 0, 1.0, -1.0).astype(F32)
    sinm = jnp.where(f == 0, nyq, jnp.sin(ang))
    fwd = jnp.stack([cosm, sinm]).astype(BF16)
    inv = jnp.stack([cosm.T, sinm.T]).astype(BF16)
    return fwd, inv


def _mm_kernel(a_ref, b_ref, o_ref):
    o_ref[0] = jnp.dot(a_ref[0], b_ref[...], preferred_element_type=F32)


def _filter_spectrum(fwd, hcat):
    _, seq, _ = fwd.shape
    n = hcat.shape[1]
    tm = min(512, seq)
    tn = 512
    return pl.pallas_call(
        _mm_kernel,
        grid=(2, seq // tm, n // tn),
        in_specs=[pl.BlockSpec((1, tm, seq), lambda h, i, j: (h, i, 0)),
                  pl.BlockSpec((seq, tn), lambda h, i, j: (0, j))],
        out_specs=pl.BlockSpec((1, tm, tn), lambda h, i, j: (h, i, j)),
        out_shape=jax.ShapeDtypeStruct((2, seq, n), F32),
        compiler_params=_params(("parallel", "parallel", "parallel"), 32),
        name="hy_filter_dft",
    )(fwd, hcat)


def _hy_fwd_kernel(f_ref, uu_ref, kf_ref, kb_ref, d_ref, y_ref, *, n_fft):
    tm = f_ref.shape[1]
    uu = uu_ref[0]
    a = jnp.dot(f_ref[0], uu, preferred_element_type=F32)
    b = jnp.dot(f_ref[1], uu, preferred_element_type=F32)
    row0 = (lax.broadcasted_iota(jnp.int32, a.shape, 0) + pl.program_id(2) * tm) == 0
    ka = kf_ref[0] + kb_ref[0] + d_ref[0]
    kb_sum = kf_ref[1] + kb_ref[1] + d_ref[0]
    kb_dif = kf_ref[1] - kb_ref[1]
    ya = jnp.where(row0, a * ka, a * ka - b * kb_dif)
    yb = jnp.where(row0, b * kb_sum, a * kb_dif + b * ka)
    wgt = jnp.where(row0, 1.0 / n_fft, 2.0 / n_fft)
    y_ref[0, 0] = (ya * wgt).astype(BF16)
    y_ref[0, 1] = (yb * wgt).astype(BF16)


def _hy_fwd(fwd, uu, kspec, layer, hyena_d):
    bsz, seq, _ = uu.shape
    ct = HY_CT
    nct = D_HYENA // ct
    tm = min(512, seq)
    return pl.pallas_call(
        functools.partial(_hy_fwd_kernel, n_fft=2 * seq),
        grid=(bsz, nct, seq // tm),
        in_specs=[pl.BlockSpec((2, tm, seq), lambda bi, j, i: (0, i, 0)),
                  pl.BlockSpec((1, seq, ct), lambda bi, j, i: (bi, 0, j)),
                  pl.BlockSpec((2, tm, ct), lambda bi, j, i: (0, i, j)),
                  pl.BlockSpec((2, tm, ct), lambda bi, j, i: (0, i, nct + j)),
                  pl.BlockSpec((1, 1, ct), lambda bi, j, i: (layer, 0, j))],
        out_specs=pl.BlockSpec((1, 2, tm, ct), lambda bi, j, i: (bi, 0, i, j)),
        out_shape=jax.ShapeDtypeStruct((bsz, 2, seq, D_HYENA), BF16),
        compiler_params=_params(("parallel", "parallel", "arbitrary"), 48),
        name="hy_fwd_dft",
    )(fwd, uu, kspec, kspec, hyena_d.reshape(DEPTH, 1, D_HYENA))


def _hy_inv_kernel(ft_ref, y_ref, x0_ref, o_ref):
    y = (jnp.dot(ft_ref[0], y_ref[0, 0], preferred_element_type=F32)
         + jnp.dot(ft_ref[1], y_ref[0, 1], preferred_element_type=F32))
    o_ref[...] = (y * x0_ref[0].astype(F32)).astype(BF16)


def _hy_inv(inv, yspec, x0, seg, m_total, prev):
    bsz, seq, rb0 = seg
    ct = HY_CT
    nct = D_HYENA // ct
    tm = min(512, seq)
    per = seq // tm
    return _seg_call(
        _hy_inv_kernel, prev=prev, n_in=3,
        grid=(bsz, nct, per),
        in_specs=[pl.BlockSpec((2, tm, seq), lambda bi, j, i: (0, i, 0)),
                  pl.BlockSpec((1, 2, seq, ct), lambda bi, j, i: (bi, 0, 0, j)),
                  pl.BlockSpec((1, tm, ct), lambda bi, j, i: (bi, i, j))],
        out_specs=pl.BlockSpec((tm, ct), lambda bi, j, i: ((rb0 + bi) * per + i, j)),
        out_shape=jax.ShapeDtypeStruct((m_total, D_HYENA), BF16),
        compiler_params=_params(("parallel", "parallel", "arbitrary"), 48),
        name="hy_inv_dft",
        args=[inv, yspec, x0],
    )


def _swap_halves(x):
    return jnp.concatenate([pltpu.roll(x[:, :LANE], LANE // 2, 1),
                            pltpu.roll(x[:, LANE:], LANE // 2, 1)], axis=1)


def _ret_kernel(*refs, use_rope, need_out):
    refs = list(refs)
    dec_ref, q_ref, k_ref, v_ref, g_ref = refs[:5]
    refs = refs[5:]
    if use_rope:
        cos_ref, sin_ref = refs[:2]
        refs = refs[2:]
    sf0_ref, sb0_ref = refs[:2]
    refs = refs[2:]
    if need_out:
        y_ref = refs[0]
        refs = refs[1:]
    sfo_ref, sbo_ref, qs, kst, accf, accb, st = refs

    seq = k_ref.shape[0]
    csz = RET_CHUNK
    n_chunks = seq // csz
    hd = RET_HEAD_DIM
    heads = range(RET_HPS)

    pos = lax.broadcasted_iota(jnp.int32, (csz, hd), 0).astype(F32)
    ii = lax.broadcasted_iota(jnp.int32, (csz, csz), 0)
    jj = lax.broadcasted_iota(jnp.int32, (csz, csz), 1)
    rel = (ii - jj).astype(F32)
    qdec_f, vdec_f, qdec_b, vdec_b, cdec_f, cdec_b, mask = [], [], [], [], [], [], []
    for hh in heads:
        lg = jnp.log1p(-jnp.exp(dec_ref[hh]))
        lgf = lg[0:1, :]
        lgb = lg[1:2, :]
        qdec_f.append(jnp.exp((pos + 1.0) * lgf))
        vdec_f.append(jnp.exp((csz - 1.0 - pos) * lgf))
        qdec_b.append(jnp.exp((csz - pos) * lgb))
        vdec_b.append(jnp.exp(pos * lgb))
        cdec_f.append(jnp.exp(csz * lgf))
        cdec_b.append(jnp.exp(csz * lgb))
        mask.append(jnp.where(rel >= 0, jnp.exp(jnp.maximum(rel, 0.0) * lgf[:, :csz]), 0.0)
                    + jnp.where(rel <= 0, jnp.exp(jnp.maximum(-rel, 0.0) * lgb[:, :csz]), 0.0))

    def chunk_rows(c):
        return pl.ds(pl.multiple_of(c * csz, csz), csz)

    def rope(x, rows):
        if not use_rope:
            return x
        return x * cos_ref[rows, :] + _swap_halves(x) * sin_ref[rows, :]

    def prep(c, carry):
        rows = chunk_rows(c)
        for hh in heads:
            cols = slice(hh * hd, (hh + 1) * hd)
            kc = rope(k_ref[rows, cols].astype(F32) * (hd ** -0.5), rows)
            kst[c, hh] = kc.T.astype(BF16)
            if need_out:
                qs[rows, cols] = rope(q_ref[rows, cols].astype(F32), rows).astype(BF16)
        return carry

    lax.fori_loop(0, n_chunks, prep, 0, unroll=2)

    for hh in heads:
        st[hh, 0] = sf0_ref[0, hh]
        st[hh, 1] = sb0_ref[0, hh]

    def scan(i, carry):
        for hh in heads:
            cols = slice(hh * hd, (hh + 1) * hd)
            for direction, c in ((0, i), (1, n_chunks - 1 - i)):
                rows = chunk_rows(c)
                kt = kst[c, hh]
                v = v_ref[rows, cols]
                state = st[hh, direction]
                if need_out:
                    q = qs[rows, cols]
                    carried = jnp.dot(q, state.astype(BF16), preferred_element_type=F32)
                    if direction == 0:
                        s = jnp.dot(q, kt, preferred_element_type=F32)
                        o = jnp.dot((s * mask[hh]).astype(BF16), v, preferred_element_type=F32)
                        accf[rows, cols] = o + qdec_f[hh] * carried
                    else:
                        accb[rows, cols] = qdec_b[hh] * carried
                vdec, cdec = (vdec_f, cdec_f) if direction == 0 else (vdec_b, cdec_b)
                st[hh, direction] = state * cdec[hh] + jnp.dot(
                    kt, (v.astype(F32) * vdec[hh]).astype(BF16), preferred_element_type=F32)
        return carry

    lax.fori_loop(0, n_chunks, scan, 0, unroll=2)

    if need_out:
        def finish(c, carry):
            rows = chunk_rows(c)
            for hh in heads:
                cols = slice(hh * hd, (hh + 1) * hd)
                o = accf[rows, cols] + accb[rows, cols]
                mu = jnp.mean(o, axis=-1, keepdims=True)
                d = o - mu
                var = jnp.mean(d * d, axis=-1, keepdims=True)
                gate = _silu(g_ref[rows, cols].astype(F32))
                y_ref[rows, cols] = (gate * (d * lax.rsqrt(var + GN_EPS))).astype(BF16)
            return carry

        lax.fori_loop(0, n_chunks, finish, 0, unroll=2)

    for hh in heads:
        sfo_ref[0, hh] = st[hh, 0]
        sbo_ref[0, hh] = st[hh, 1]


def _retention(z, seg, layer, dec_all, rope, s_f, s_b, need_out, prev):
    bsz, seq, rb0 = seg
    hd = RET_HEAD_DIM
    hps = RET_HPS
    wide = hps * hd
    use_rope = rope is not None
    col = lambda off: pl.BlockSpec((seq, wide), lambda bi, h: (rb0 + bi, off // wide + h))
    state = pl.BlockSpec((1, hps, hd, hd), lambda bi, h: (bi, h, 0, 0))
    in_specs = [pl.BlockSpec((None, hps, 2, hd), lambda bi, h: (layer, h, 0, 0)),
                col(O_Q), col(O_K), col(O_V), col(O_G)]
    args = [dec_all, z, z, z, z]
    if use_rope:
        in_specs += [pl.BlockSpec((seq, hd), lambda bi, h: (0, 0))] * 2
        args += list(rope)
    in_specs += [state, state]
    args += [s_f, s_b]
    out_specs = [state, state]
    out_shape = [jax.ShapeDtypeStruct((bsz, RET_HEADS, hd, hd), F32)] * 2
    n_chunks = seq // RET_CHUNK
    scratch = [pltpu.VMEM((seq, wide), BF16), pltpu.VMEM((n_chunks, hps, hd, RET_CHUNK), BF16),
               pltpu.VMEM((seq, wide), F32), pltpu.VMEM((seq, wide), F32),
               pltpu.VMEM((hps, 2, hd, hd), F32)]
    kw = dict(grid=(bsz, RET_HEADS // hps), scratch_shapes=scratch,
              compiler_params=_params(("parallel", "parallel"), 56))
    kern = functools.partial(_ret_kernel, use_rope=use_rope, need_out=need_out)
    if not need_out:
        res = pl.pallas_call(kern, in_specs=in_specs, out_specs=out_specs, out_shape=out_shape,
                             name="retention_state", **kw)(*args)
        return None, res[0], res[1]
    out_specs = [pl.BlockSpec((seq, wide), lambda bi, h: (rb0 + bi, h))] + out_specs
    out_shape = [jax.ShapeDtypeStruct((z.shape[0], D_RET), BF16)] + out_shape
    res = _seg_call(kern, prev=prev, n_in=len(args), in_specs=in_specs, out_specs=out_specs,
                    out_shape=out_shape, name="retention", args=args, **kw)
    return res[0], res[1], res[2]


def _rope_tables(seq):
    rows = seq // GRID_W
    row = jnp.repeat(jnp.arange(rows, dtype=F32), GRID_W)
    colp = jnp.tile(jnp.arange(GRID_W, dtype=F32), rows)
    inv = ROPE_BASE ** (-jnp.arange(ROPE_PAIRS, dtype=F32) / ROPE_PAIRS)
    ang_r = row[:, None] * inv[None, :]
    ang_c = colp[:, None] * inv[None, :]
    cr, sr, cc, sc = jnp.cos(ang_r), jnp.sin(ang_r), jnp.cos(ang_c), jnp.sin(ang_c)
    return (jnp.concatenate([cr, cr, cc, cc], axis=-1),
            jnp.concatenate([-sr, sr, -sc, sc], axis=-1))


def _merge_kernel(ya_ref, yb_ref, yc_ref, ga_ref, gb_ref, gc_ref, pa_ref, pb_ref, pc_ref, o_ref):
    def branch(g_ref, y_ref, p_ref):
        return (jax.nn.sigmoid(g_ref[...].astype(F32))
                * jnp.dot(y_ref[...], p_ref[0], preferred_element_type=F32))

    m = branch(ga_ref, ya_ref, pa_ref) + branch(gb_ref, yb_ref, pb_ref) + branch(gc_ref, yc_ref, pc_ref)
    o_ref[...] = m.astype(BF16)


def _merge(ya, yb, yc, z, layer, p_a, p_b, p_c, m_rows):
    tm, tn = TM_WS, TN_WS
    goff = O_GATE // tn
    gstep = D_MODEL // tn
    act = lambda width: pl.BlockSpec((tm, width), lambda i, j: (i, 0))
    gate = lambda br: pl.BlockSpec((tm, tn), lambda i, j: (i, goff + br * gstep + j))
    wgt = lambda rows: pl.BlockSpec((1, rows, tn), lambda i, j: (layer, 0, j))
    return pl.pallas_call(
        _merge_kernel,
        grid=(m_rows // tm, D_MODEL // tn),
        in_specs=[act(D_POOL), act(D_HYENA), act(D_RET), gate(0), gate(1), gate(2),
                  wgt(D_POOL), wgt(D_HYENA), wgt(D_RET)],
        out_specs=pl.BlockSpec((tm, tn), lambda i, j: (i, j)),
        out_shape=jax.ShapeDtypeStruct((z.shape[0], D_MODEL), BF16),
        compiler_params=_params(("parallel", "arbitrary"), 48),
        name="merge",
    )(ya, yb, yc, z, z, z, p_a, p_b, p_c)


def _res_ln_kernel(*refs, emit_xm):
    a_ref, w_ref, b_ref, x_ref, gt_ref, g_ref, be_ref = refs[:7]
    refs = refs[7:]
    if emit_xm:
        sh_ref, sc_ref, xo_ref, xm_ref, acc_ref = refs
    else:
        xo_ref, acc_ref = refs
    k = pl.program_id(1)

    @pl.when(k == 0)
    def _():
        acc_ref[...] = jnp.zeros_like(acc_ref)

    acc_ref[...] += jnp.dot(a_ref[...], w_ref[0], preferred_element_type=F32)

    @pl.when(k == pl.num_programs(1) - 1)
    def _():
        r = DEEPNORM_ALPHA * x_ref[...] + gt_ref[0] * (acc_ref[...] + b_ref[0])
        xn = _layer_norm(r, g_ref[0], be_ref[0])
        xo_ref[...] = xn
        if emit_xm:
            xm_ref[...] = (xn * (1.0 + sc_ref[0]) + sh_ref[0]).astype(BF16)


def _res_ln(a, w_all, b_all, x, mod, layer, gate_blk, ln_g, ln_b, next_mod, m_rows, seq, bsz, name):
    kdim = a.shape[1]
    tm = TM_ROW
    tk = min(1024, kdim)
    per = seq // tm
    gidx = _mod_index(layer, per, bsz)
    vec = lambda arr: arr.reshape(DEPTH, 1, D_MODEL)
    lvec = pl.BlockSpec((1, 1, D_MODEL), lambda i, k: (layer, 0, 0))
    in_specs = [pl.BlockSpec((tm, tk), lambda i, k: (i, k)),
                pl.BlockSpec((1, tk, D_MODEL), lambda i, k: (layer, k, 0)),
                lvec,
                pl.BlockSpec((tm, D_MODEL), lambda i, k: (i, 0)),
                pl.BlockSpec((1, 1, D_MODEL), lambda i, k: (gidx(i), 0, gate_blk)),
                lvec, lvec]
    args = [a, w_all, vec(b_all), x, mod, vec(ln_g), vec(ln_b)]
    row_out = pl.BlockSpec((tm, D_MODEL), lambda i, k: (i, 0))
    out_specs = [row_out]
    out_shape = [jax.ShapeDtypeStruct((m_rows, D_MODEL), F32)]
    if next_mod is not None:
        nl, sh_blk, sc_blk = next_mod
        nidx = _mod_index(nl, per, bsz)
        in_specs += [pl.BlockSpec((1, 1, D_MODEL), lambda i, k: (nidx(i), 0, sh_blk)),
                     pl.BlockSpec((1, 1, D_MODEL), lambda i, k: (nidx(i), 0, sc_blk))]
        args += [mod, mod]
        out_specs.append(row_out)
        out_shape.append(jax.ShapeDtypeStruct((m_rows, D_MODEL), BF16))
    res = pl.pallas_call(
        functools.partial(_res_ln_kernel, emit_xm=next_mod is not None),
        grid=(m_rows // tm, kdim // tk),
        in_specs=in_specs,
        out_specs=out_specs,
        out_shape=out_shape,
        scratch_shapes=[pltpu.VMEM((tm, D_MODEL), F32)],
        compiler_params=_params(("parallel", "arbitrary"), 48),
        name=name,
    )(*args)
    return (res[0], res[1]) if next_mod is not None else (res[0], None)


def kernel(x, c, ctx, c_ctx, w_ada, b_ada, w_in, b_in, conv_w, conv_b, pool_w, pool_scale, filt_w1, filt_b1, filt_f1, filt_w2, filt_b2, filt_f2, filt_w3, filt_b3, filt_f3, filt_w4, hyena_d, ret_decay, p_a, p_b, p_c, w_o, b_o, ln1_g, ln1_b, w_mlp1, b_mlp1, w_mlp2, b_mlp2, ln2_g, ln2_b):
    bsz, seq, _ = x.shape
    ctx_len = ctx.shape[1]
    assert x.shape == (bsz, seq, D_MODEL) and ctx.shape == (bsz, ctx_len, D_MODEL)
    assert seq % RET_CHUNK == 0 and ctx_len % RET_CHUNK == 0 and seq % GRID_W == 0
    assert bsz + 1 <= ADA_ROWS and seq % ctx_len == 0
    n_lat, n_ctx = bsz * seq, bsz * ctx_len
    m_total = n_lat + n_ctx
    assert seq % TM_WS == 0 and n_ctx % TM_WS == 0 and seq % TM_ROW == 0 and n_ctx % TM_ROW == 0
    seg_x = (bsz, seq, 0)
    seg_c = (bsz, ctx_len, n_lat // ctx_len)

    cvec = jnp.concatenate([c, c_ctx[None, :], jnp.zeros((ADA_ROWS - bsz - 1, D_MODEL), F32)], axis=0)
    mod = _ada(cvec, w_ada, b_ada).reshape(DEPTH * ADA_ROWS, 1, 6 * D_MODEL)

    deltas = _filter_deltas()
    fwd_x, inv_x = _dft_matrices(seq)
    fwd_c, inv_c = _dft_matrices(ctx_len)
    zfeat_x, zfeat_c = _filter_features(seq), _filter_features(ctx_len)
    rope = _rope_tables(seq)
    zero_state = jnp.zeros((bsz, RET_HEADS, RET_HEAD_DIM, RET_HEAD_DIM), F32)

    row3 = lambda a: a.reshape(DEPTH, 1, -1)
    fp = {'w1': jnp.pad(filt_w1, ((0, 0), (0, FEAT_PAD - FILTER_EMB), (0, 0))),
          'b1': row3(filt_b1), 'f1': row3(filt_f1), 'w2': filt_w2, 'b2': row3(filt_b2), 'f2': row3(filt_f2),
          'w3': filt_w3, 'b3': row3(filt_b3), 'f3': row3(filt_f3), 'w4': filt_w4}
    dec_all = jnp.broadcast_to(jnp.swapaxes(ret_decay, 1, 2)[:, :, :, None],
                               (DEPTH, RET_HEADS, 2, RET_HEAD_DIM))
    pa_b, pb_b, pc_b = p_a.astype(BF16), p_b.astype(BF16), p_c.astype(BF16)
    wo_b, w2_b = w_o.astype(BF16), w_mlp2.astype(BF16)

    xs, xm = _mod0(x.reshape(n_lat, D_MODEL), ctx.reshape(n_ctx, D_MODEL), mod, seq, bsz)

    def hyena(z, seg, l, fwd, inv, zfeat, prev):
        uu, x0 = _hy_pre(z, seg, l, conv_w, conv_b)
        kspec = _filter_spectrum(fwd, _filters(zfeat, deltas, l, fp))
        return _hy_inv(inv, _hy_fwd(fwd, uu, kspec, l, hyena_d), x0, seg, m_total, prev)

    for l in range(DEPTH):
        last = l == DEPTH - 1
        rows = n_lat if last else m_total
        z = _wsmm(xm, w_in, b_in, l, m_total, False, "in_proj")
        if not last:
            ya = _pool(z, seg_c, l, pool_w, pool_scale, None)
            yb = hyena(z, seg_c, l, fwd_c, inv_c, zfeat_c, None)
            yc, s_f, s_b = _retention(z, seg_c, l, dec_all, None, zero_state, zero_state, True, None)
        else:
            ya = yb = None
            yc, s_f, s_b = _retention(z, seg_c, l, dec_all, None, zero_state, zero_state, False, None)
        ya = _pool(z, seg_x, l, pool_w, pool_scale, ya)
        yb = hyena(z, seg_x, l, fwd_x, inv_x, zfeat_x, yb)
        yc, _, _ = _retention(z, seg_x, l, dec_all, rope, s_f, s_b, True, yc)
        merged = _merge(ya, yb, yc, z, l, pa_b, pb_b, pc_b, rows)
        xs, xm = _res_ln(merged, wo_b, b_o, xs, mod, l, 2, ln1_g, ln1_b, (l, 3, 4), rows, seq, bsz,
                         "out_proj_ln1")
        hid = _wsmm(xm, w_mlp1, b_mlp1, l, rows, True, "mlp_up")
        xs, xm = _res_ln(hid, w2_b, b_mlp2, xs, mod, l, 5, ln2_g, ln2_b,
                         None if last else (l + 1, 0, 1), rows, seq, bsz, "mlp_down_ln2")
    return xs.reshape(bsz, seq, D_MODEL)
```

```python
import functools
import math

import jax
import jax.numpy as jnp
from jax import lax
from jax.experimental import pallas as pl
from jax.experimental.pallas import tpu as pltpu

F32 = jnp.float32
BF16 = jnp.bfloat16

D_MODEL = 2048
DEPTH = 4
GRID_W = 64
D_POOL = D_MODEL // 4
POOL_WINDOWS = (2, 4, 8, 16)
POOL_GROUP = D_POOL // len(POOL_WINDOWS)
D_HYENA = D_MODEL // 4
FILTER_EMB = 33
FILTER_BANDS = (FILTER_EMB - 1) // 2
FILTER_ORDER = 64
FILTER_DECAY_TARGET = 1e-2
FILTER_FAST_PCT = 0.3
FILTER_SLOW_PCT = 1.5
RET_HEAD_DIM = 256
D_RET = D_MODEL // 2
RET_HEADS = D_RET // RET_HEAD_DIM
RET_CHUNK = 128
ROPE_BASE = 10000.0
ROPE_PAIRS = RET_HEAD_DIM // 4
N_BRANCH = 3
D_FF = 4 * D_MODEL
LN_EPS = 1e-5
GN_EPS = 1e-6
DEEPNORM_ALPHA = (2 * DEPTH) ** 0.25
O_POOL = 0
O_HY = O_POOL + D_POOL
O_Q = O_HY + 3 * D_HYENA
O_K = O_Q + D_RET
O_V = O_K + D_RET
O_G = O_V + D_RET
O_GATE = O_G + D_RET
D_IN = O_GATE + N_BRANCH * D_MODEL

LANE = 128
FEAT_PAD = LANE
POOL_PAD = 16
ADA_ROWS = 8
MIB = 1024 * 1024

TM_WS = 1024
TN_WS = 1024
TM_ROW = 512
HY_CT = 256
RET_HPS = 2
TK_ROW = 2048
RES_SUB = 4
MERGE_SUB = 2
LN_ROWS = 16


def _params(semantics, vmem_mib):
    return pltpu.CompilerParams(dimension_semantics=semantics, vmem_limit_bytes=vmem_mib * MIB)


def _silu(v):
    return v * jax.nn.sigmoid(v)


def _layer_norm(r, g, b):
    mu = jnp.mean(r, axis=-1, keepdims=True)
    d = r - mu
    var = jnp.mean(d * d, axis=-1, keepdims=True)
    return d * lax.rsqrt(var + LN_EPS) * g + b


def _skip_ref(kern, idx):
    def wrapped(*refs):
        return kern(*refs[:idx], *refs[idx + 1:])
    return wrapped


def _seg_call(kern, *, prev, n_in, **kw):
    in_specs = list(kw.pop("in_specs"))
    args = list(kw.pop("args"))
    if prev is None:
        return pl.pallas_call(kern, in_specs=in_specs, **kw)(*args)
    in_specs.append(pl.BlockSpec(memory_space=pl.ANY))
    args.append(prev)
    return pl.pallas_call(_skip_ref(kern, n_in), in_specs=in_specs,
                          input_output_aliases={n_in: 0}, **kw)(*args)


def _mod_index(layer, per_batch_tiles, bsz):
    return lambda i: layer * ADA_ROWS + jnp.minimum(i // per_batch_tiles, bsz)


def _ada_kernel(c_ref, w_ref, b_ref, o_ref):
    s = _silu(c_ref[...]).astype(BF16)
    o_ref[0] = jnp.dot(s, w_ref[0].astype(BF16), preferred_element_type=F32) + b_ref[0]


def _ada(cvec, w_ada, b_ada):
    tn = 1024
    n = w_ada.shape[2]
    return pl.pallas_call(
        _ada_kernel,
        grid=(DEPTH, n // tn),
        in_specs=[
            pl.BlockSpec((ADA_ROWS, D_MODEL), lambda l, j: (0, 0)),
            pl.BlockSpec((1, D_MODEL, tn), lambda l, j: (l, 0, j)),
            pl.BlockSpec((1, 1, tn), lambda l, j: (l, 0, j)),
        ],
        out_specs=pl.BlockSpec((1, ADA_ROWS, tn), lambda l, j: (l, 0, j)),
        out_shape=jax.ShapeDtypeStruct((DEPTH, ADA_ROWS, n), F32),
        compiler_params=_params(("parallel", "parallel"), 40),
        name="ada",
    )(cvec, w_ada, b_ada.reshape(DEPTH, 1, n))


def _mod0_kernel(x_ref, c_ref, sh_ref, sc_ref, xo_ref, xm_ref, *, n_lat_tiles):
    def emit(v):
        xo_ref[...] = v
        xm_ref[...] = (v * (1.0 + sc_ref[0]) + sh_ref[0]).astype(BF16)

    i = pl.program_id(0)

    @pl.when(i < n_lat_tiles)
    def _():
        emit(x_ref[...])

    @pl.when(i >= n_lat_tiles)
    def _():
        emit(c_ref[...])


def _mod0(x2d, c2d, mod, seq, bsz):
    n_lat, n_ctx = x2d.shape[0], c2d.shape[0]
    tm = TM_ROW
    nl = n_lat // tm
    midx = _mod_index(0, seq // tm, bsz)
    return pl.pallas_call(
        functools.partial(_mod0_kernel, n_lat_tiles=nl),
        grid=((n_lat + n_ctx) // tm,),
        in_specs=[pl.BlockSpec((tm, D_MODEL), lambda i: (jnp.minimum(i, nl - 1), 0)),
                  pl.BlockSpec((tm, D_MODEL), lambda i: (jnp.maximum(i - nl, 0), 0)),
                  pl.BlockSpec((1, 1, D_MODEL), lambda i: (midx(i), 0, 0)),
                  pl.BlockSpec((1, 1, D_MODEL), lambda i: (midx(i), 0, 1))],
        out_specs=[pl.BlockSpec((tm, D_MODEL), lambda i: (i, 0)),
                   pl.BlockSpec((tm, D_MODEL), lambda i: (i, 0))],
        out_shape=[jax.ShapeDtypeStruct((n_lat + n_ctx, D_MODEL), F32),
                   jax.ShapeDtypeStruct((n_lat + n_ctx, D_MODEL), BF16)],
        compiler_params=_params(("parallel",), 40),
        name="assemble_modulate",
    )(x2d, c2d, mod, mod)


def _wsmm_kernel(*refs, sq_relu, with_side):
    if with_side:
        x_ref, w_ref, b_ref, s_ref, o_ref, so_ref, wb_ref = refs
        so_ref[...] = s_ref[0].astype(BF16)
    else:
        x_ref, w_ref, b_ref, o_ref, wb_ref = refs

    @pl.when(pl.program_id(1) == 0)
    def _():
        wb_ref[...] = w_ref[0].astype(BF16)

    y = jnp.dot(x_ref[...], wb_ref[...], preferred_element_type=F32) + b_ref[0]
    if sq_relu:
        y = jnp.square(jnp.maximum(y, 0.0))
    o_ref[...] = y.astype(BF16)


def _wsmm(xm, w_all, b_all, layer, m_rows, sq_relu, name, side=None):
    kdim, n = w_all.shape[1], w_all.shape[2]
    tm, tn = TM_WS, TN_WS
    n_i = m_rows // tm
    in_specs = [pl.BlockSpec((tm, kdim), lambda j, i: (i, 0)),
                pl.BlockSpec((1, kdim, tn), lambda j, i: (layer, 0, j)),
                pl.BlockSpec((1, 1, tn), lambda j, i: (layer, 0, j))]
    args = [xm, w_all, b_all.reshape(DEPTH, 1, n)]
    out_specs = [pl.BlockSpec((tm, tn), lambda j, i: (i, j))]
    out_shape = [jax.ShapeDtypeStruct((xm.shape[0], n), BF16)]
    if side is not None:
        slabs = min(n_i, 8)
        rows = tn // slabs
        sidx = lambda j, i: j * slabs + jnp.minimum(i, slabs - 1)
        in_specs.append(pl.BlockSpec((1, rows, side.shape[2]), lambda j, i: (layer, sidx(j, i), 0)))
        args.append(side)
        out_specs.append(pl.BlockSpec((rows, side.shape[2]), lambda j, i: (sidx(j, i), 0)))
        out_shape.append(jax.ShapeDtypeStruct(side.shape[1:], BF16))
    res = pl.pallas_call(
        functools.partial(_wsmm_kernel, sq_relu=sq_relu, with_side=side is not None),
        grid=(n // tn, n_i),
        in_specs=in_specs,
        out_specs=out_specs,
        out_shape=out_shape,
        scratch_shapes=[pltpu.VMEM((kdim, tn), BF16)],
        compiler_params=_params(("parallel", "arbitrary"), 48),
        name=name,
    )(*args)
    return res if side is not None else res[0]


def _pool_kernel(z_ref, w_ref, s_ref, o_ref, pad_ref):
    seq = z_ref.shape[0]
    rows = seq + 2 * POOL_PAD
    zeros = jnp.zeros((POOL_PAD, POOL_GROUP), F32)
    pad_ref[0:POOL_PAD, :] = zeros
    pad_ref[POOL_PAD + seq:rows, :] = zeros
    t = lax.broadcasted_iota(jnp.int32, (seq, POOL_GROUP), 0)
    for g, win in enumerate(POOL_WINDOWS):
        cols = slice(g * POOL_GROUP, (g + 1) * POOL_GROUP)
        u = z_ref[:, cols].astype(F32)
        pad_ref[POOL_PAD:POOL_PAD + seq, :] = u
        w = pad_ref[...]
        w = pltpu.roll(w, 1, 0) + w
        width = 2
        while width < win:
            half = width // 2
            w = pltpu.roll(w, half, 0) + pltpu.roll(w, rows - half, 0)
            width *= 2
        half = win // 2
        count = jnp.minimum(t + half, seq) - jnp.maximum(t - half, 0)
        pooled = w[POOL_PAD:POOL_PAD + seq, :] / count.astype(F32) - u
        y = jnp.dot(pooled.astype(BF16), w_ref[0, g].astype(BF16), preferred_element_type=F32)
        o_ref[:, cols] = (y * s_ref[0, :, cols]).astype(BF16)


def _pool(z, seg, layer, pool_w, pool_scale, prev):
    bsz, seq, rb0 = seg
    ng = len(POOL_WINDOWS)
    return _seg_call(
        _pool_kernel, prev=prev, n_in=3,
        grid=(bsz,),
        in_specs=[pl.BlockSpec((seq, D_POOL), lambda bi: (rb0 + bi, O_POOL // D_POOL)),
                  pl.BlockSpec((1, ng, POOL_GROUP, POOL_GROUP), lambda bi: (layer, 0, 0, 0)),
                  pl.BlockSpec((1, 1, D_POOL), lambda bi: (layer, 0, 0))],
        out_specs=pl.BlockSpec((seq, D_POOL), lambda bi: (rb0 + bi, 0)),
        out_shape=jax.ShapeDtypeStruct((z.shape[0], D_POOL), BF16),
        scratch_shapes=[pltpu.VMEM((seq + 2 * POOL_PAD, POOL_GROUP), F32)],
        compiler_params=_params(("parallel",), 48),
        name="pool",
        args=[z, pool_w, pool_scale.reshape(DEPTH, 1, D_POOL)],
    )


def _conv3(u, w, b):
    seq = u.shape[0]
    t = lax.broadcasted_iota(jnp.int32, u.shape, 0)
    prev = jnp.where(t == 0, 0.0, pltpu.roll(u, 1, 0))
    nxt = jnp.where(t == seq - 1, 0.0, pltpu.roll(u, seq - 1, 0))
    return prev * w[0:1, :] + u * w[1:2, :] + nxt * w[2:3, :] + b


def _hy_pre_kernel(zv_ref, z0_ref, z1_ref, wv_ref, w0_ref, w1_ref, bv_ref, b0_ref, b1_ref,
                   uu_ref, x0_ref):
    v = _conv3(zv_ref[...].astype(F32), wv_ref[0], bv_ref[0])
    x1 = _conv3(z1_ref[...].astype(F32), w1_ref[0], b1_ref[0])
    uu_ref[0] = (v * x1).astype(BF16)
    x0_ref[0] = _conv3(z0_ref[...].astype(F32), w0_ref[0], b0_ref[0]).astype(BF16)


def _hy_pre(z, seg, layer, conv_w, conv_b):
    bsz, seq, rb0 = seg
    ct = HY_CT
    nct = D_HYENA // ct
    zoff = O_HY // ct
    zspec = [pl.BlockSpec((seq, ct), functools.partial(lambda bi, j, s: (rb0 + bi, zoff + s * nct + j), s=s))
             for s in range(3)]
    wspec = [pl.BlockSpec((1, 3, ct), functools.partial(lambda bi, j, s: (layer, 0, s * nct + j), s=s))
             for s in range(3)]
    bspec = [pl.BlockSpec((1, 1, ct), functools.partial(lambda bi, j, s: (layer, 0, s * nct + j), s=s))
             for s in range(3)]
    cb = conv_b.reshape(DEPTH, 1, 3 * D_HYENA)
    out = pl.BlockSpec((1, seq, ct), lambda bi, j: (bi, 0, j))
    return pl.pallas_call(
        _hy_pre_kernel,
        grid=(bsz, nct),
        in_specs=zspec + wspec + bspec,
        out_specs=[out, out],
        out_shape=[jax.ShapeDtypeStruct((bsz, seq, D_HYENA), BF16)] * 2,
        compiler_params=_params(("parallel", "parallel"), 48),
        name="hy_pre",
    )(z, z, z, conv_w, conv_w, conv_w, cb, cb, cb)


def _filt_kernel(zf_ref, w1_ref, b1_ref, f1_ref, w2_ref, b2_ref, f2_ref, w3_ref, b3_ref, f3_ref,
                 w4_ref, dl_ref, o_ref):
    tl = zf_ref.shape[0]
    zf = zf_ref[...]

    def dense(a, w_ref):
        return jnp.dot(a.astype(BF16), w_ref[0].astype(BF16), preferred_element_type=F32)

    hdn = jnp.sin(f1_ref[0] * (dense(zf, w1_ref) + b1_ref[0]))
    hdn = jnp.sin(f2_ref[0] * (dense(hdn, w2_ref) + b2_ref[0]))
    hdn = jnp.sin(f3_ref[0] * (dense(hdn, w3_ref) + b3_ref[0]))
    h = dense(hdn, w4_ref)
    decay = jnp.exp(-zf[:, 0:1] * jnp.abs(dl_ref[...]))
    row = lax.broadcasted_iota(jnp.int32, (tl, D_HYENA), 0) + pl.program_id(0) * tl
    o_ref[:, 0:D_HYENA] = (h[:, 0:D_HYENA] * decay).astype(BF16)
    o_ref[:, D_HYENA:] = jnp.where(row == 0, 0.0, h[:, D_HYENA:] * decay).astype(BF16)


def _filter_features(seq):
    t = jnp.linspace(0.0, 1.0, seq, dtype=F32)[:, None]
    w = 2.0 * math.pi * jnp.arange(seq, dtype=F32)[:, None] / seq
    f = jnp.linspace(1e-4, FILTER_BANDS - 1, FILTER_BANDS, dtype=F32)[None, :]
    z = jnp.concatenate([t, jnp.cos(f * w), -jnp.sin(f * w)], axis=-1)
    return jnp.pad(z, ((0, 0), (0, FEAT_PAD - FILTER_EMB)))


def _filter_deltas():
    max_decay = math.log(FILTER_DECAY_TARGET) / FILTER_FAST_PCT
    min_decay = math.log(FILTER_DECAY_TARGET) / FILTER_SLOW_PCT
    return jnp.linspace(min_decay, max_decay, D_HYENA, dtype=F32)[None, :]


def _filters(zfeat, deltas, layer, fp):
    seq = zfeat.shape[0]
    tl = min(256, seq)
    lsel = lambda shape: pl.BlockSpec((1,) + shape, lambda i: (layer,) + (0,) * len(shape))
    vec = lsel((1, FILTER_ORDER))
    sq = lsel((FILTER_ORDER, FILTER_ORDER))
    return pl.pallas_call(
        _filt_kernel,
        grid=(seq // tl,),
        in_specs=[pl.BlockSpec((tl, FEAT_PAD), lambda i: (i, 0)),
                  lsel((FEAT_PAD, FILTER_ORDER)), vec, vec, sq, vec, vec, sq, vec, vec,
                  lsel((FILTER_ORDER, 2 * D_HYENA)),
                  pl.BlockSpec((1, D_HYENA), lambda i: (0, 0))],
        out_specs=pl.BlockSpec((tl, 2 * D_HYENA), lambda i: (i, 0)),
        out_shape=jax.ShapeDtypeStruct((seq, 2 * D_HYENA), BF16),
        compiler_params=_params(("parallel",), 32),
        name="hy_filter",
    )(zfeat, fp['w1'], fp['b1'], fp['f1'], fp['w2'], fp['b2'], fp['f2'], fp['w3'], fp['b3'], fp['f3'],
      fp['w4'], deltas)


def _dft_matrices(seq):
    n = 2 * seq
    f = jnp.arange(seq, dtype=jnp.int32)[:, None]
    t = jnp.arange(seq, dtype=jnp.int32)[None, :]
    ang = ((f * t) % n).astype(F32) * (2.0 * math.pi / n)
    cosm = jnp.cos(ang)
    nyq = jnp.where(t % 2 == 0, 1.0, -1.0).astype(F32)
    sinm = jnp.where(f == 0, nyq, jnp.sin(ang))
    fwd = jnp.stack([cosm, sinm]).astype(BF16)
    inv = jnp.stack([cosm.T, sinm.T]).astype(BF16)
    return fwd, inv


def _mm_kernel(a_ref, b_ref, o_ref):
    o_ref[0] = jnp.dot(a_ref[0], b_ref[...], preferred_element_type=F32)


def _filter_spectrum(fwd, hcat):
    _, seq, _ = fwd.shape
    n = hcat.shape[1]
    tm = min(512, seq)
    tn = 512
    return pl.pallas_call(
        _mm_kernel,
        grid=(2, seq // tm, n // tn),
        in_specs=[pl.BlockSpec((1, tm, seq), lambda h, i, j: (h, i, 0)),
                  pl.BlockSpec((seq, tn), lambda h, i, j: (0, j))],
        out_specs=pl.BlockSpec((1, tm, tn), lambda h, i, j: (h, i, j)),
        out_shape=jax.ShapeDtypeStruct((2, seq, n), F32),
        compiler_params=_params(("parallel", "parallel", "parallel"), 32),
        name="hy_filter_dft",
    )(fwd, hcat)


def _hy_fwd_kernel(f_ref, uu_ref, kf_ref, kb_ref, d_ref, y_ref, *, n_fft):
    tm = f_ref.shape[1]
    uu = uu_ref[0]
    a = jnp.dot(f_ref[0], uu, preferred_element_type=F32)
    b = jnp.dot(f_ref[1], uu, preferred_element_type=F32)
    row0 = (lax.broadcasted_iota(jnp.int32, a.shape, 0) + pl.program_id(0) * tm) == 0
    ka = kf_ref[0] + kb_ref[0] + d_ref[0]
    kb_sum = kf_ref[1] + kb_ref[1] + d_ref[0]
    kb_dif = kf_ref[1] - kb_ref[1]
    ya = jnp.where(row0, a * ka, a * ka - b * kb_dif)
    yb = jnp.where(row0, b * kb_sum, a * kb_dif + b * ka)
    wgt = jnp.where(row0, 1.0 / n_fft, 2.0 / n_fft)
    y_ref[0, 0] = (ya * wgt).astype(BF16)
    y_ref[0, 1] = (yb * wgt).astype(BF16)


def _hy_fwd(fwd, uu, kspec, layer, hyena_d):
    bsz, seq, _ = uu.shape
    ct = HY_CT
    nct = D_HYENA // ct
    tm = min(512, seq)
    return pl.pallas_call(
        functools.partial(_hy_fwd_kernel, n_fft=2 * seq),
        grid=(seq // tm, nct, bsz),
        in_specs=[pl.BlockSpec((2, tm, seq), lambda i, j, bi: (0, i, 0)),
                  pl.BlockSpec((1, seq, ct), lambda i, j, bi: (bi, 0, j)),
                  pl.BlockSpec((2, tm, ct), lambda i, j, bi: (0, i, j)),
                  pl.BlockSpec((2, tm, ct), lambda i, j, bi: (0, i, nct + j)),
                  pl.BlockSpec((1, 1, ct), lambda i, j, bi: (layer, 0, j))],
        out_specs=pl.BlockSpec((1, 2, tm, ct), lambda i, j, bi: (bi, 0, i, j)),
        out_shape=jax.ShapeDtypeStruct((bsz, 2, seq, D_HYENA), BF16),
        compiler_params=_params(("parallel", "parallel", "parallel"), 48),
        name="hy_fwd_dft",
    )(fwd, uu, kspec, kspec, hyena_d.reshape(DEPTH, 1, D_HYENA))


def _hy_inv_kernel(ft_ref, y_ref, x0_ref, o_ref):
    y = (jnp.dot(ft_ref[0], y_ref[0, 0], preferred_element_type=F32)
         + jnp.dot(ft_ref[1], y_ref[0, 1], preferred_element_type=F32))
    o_ref[...] = (y * x0_ref[0].astype(F32)).astype(BF16)


def _hy_inv(inv, yspec, x0, seg, m_total, prev):
    bsz, seq, rb0 = seg
    ct = HY_CT
    nct = D_HYENA // ct
    tm = min(512, seq)
    per = seq // tm
    return _seg_call(
        _hy_inv_kernel, prev=prev, n_in=3,
        grid=(per, nct, bsz),
        in_specs=[pl.BlockSpec((2, tm, seq), lambda i, j, bi: (0, i, 0)),
                  pl.BlockSpec((1, 2, seq, ct), lambda i, j, bi: (bi, 0, 0, j)),
                  pl.BlockSpec((1, tm, ct), lambda i, j, bi: (bi, i, j))],
        out_specs=pl.BlockSpec((tm, ct), lambda i, j, bi: ((rb0 + bi) * per + i, j)),
        out_shape=jax.ShapeDtypeStruct((m_total, D_HYENA), BF16),
        compiler_params=_params(("parallel", "parallel", "parallel"), 48),
        name="hy_inv_dft",
        args=[inv, yspec, x0],
    )


def _swap_halves(x):
    return jnp.concatenate([pltpu.roll(x[:, :LANE], LANE // 2, 1),
                            pltpu.roll(x[:, LANE:], LANE // 2, 1)], axis=1)


def _ret_kernel(*refs, use_rope, need_out):
    refs = list(refs)
    dec_ref, q_ref, k_ref, v_ref, g_ref = refs[:5]
    refs = refs[5:]
    if use_rope:
        cos_ref, sin_ref = refs[:2]
        refs = refs[2:]
    sf0_ref, sb0_ref = refs[:2]
    refs = refs[2:]
    if need_out:
        y_ref = refs[0]
        refs = refs[1:]
    sfo_ref, sbo_ref, qs, kst, accf, accb, st = refs

    seq = k_ref.shape[0]
    csz = RET_CHUNK
    n_chunks = seq // csz
    hd = RET_HEAD_DIM
    heads = range(RET_HPS)

    pos = lax.broadcasted_iota(jnp.int32, (csz, hd), 0).astype(F32)
    ii = lax.broadcasted_iota(jnp.int32, (csz, csz), 0)
    jj = lax.broadcasted_iota(jnp.int32, (csz, csz), 1)
    rel = (ii - jj).astype(F32)
    qdec_f, vdec_f, qdec_b, vdec_b, cdec_f, cdec_b, mask = [], [], [], [], [], [], []
    for hh in heads:
        lg = jnp.log1p(-jnp.exp(dec_ref[hh]))
        lgf = lg[0:1, :]
        lgb = lg[1:2, :]
        qdec_f.append(jnp.exp((pos + 1.0) * lgf))
        vdec_f.append(jnp.exp((csz - 1.0 - pos) * lgf))
        qdec_b.append(jnp.exp((csz - pos) * lgb))
        vdec_b.append(jnp.exp(pos * lgb))
        cdec_f.append(jnp.exp(csz * lgf))
        cdec_b.append(jnp.exp(csz * lgb))
        mask.append(jnp.where(rel >= 0, jnp.exp(jnp.maximum(rel, 0.0) * lgf[:, :csz]), 0.0)
                    + jnp.where(rel <= 0, jnp.exp(jnp.maximum(-rel, 0.0) * lgb[:, :csz]), 0.0))

    def chunk_rows(c):
        return pl.ds(pl.multiple_of(c * csz, csz), csz)

    def rope(x, rows):
        if not use_rope:
            return x
        return x * cos_ref[rows, :] + _swap_halves(x) * sin_ref[rows, :]

    def prep(c, carry):
        rows = chunk_rows(c)
        for hh in heads:
            cols = slice(hh * hd, (hh + 1) * hd)
            kc = rope(k_ref[rows, cols].astype(F32) * (hd ** -0.5), rows)
            kst[c, hh] = kc.T.astype(BF16)
            if need_out:
                qs[rows, cols] = rope(q_ref[rows, cols].astype(F32), rows).astype(BF16)
        return carry

    lax.fori_loop(0, n_chunks, prep, 0, unroll=2)

    for hh in heads:
        st[hh, 0] = sf0_ref[0, hh]
        st[hh, 1] = sb0_ref[0, hh]

    def scan(i, carry):
        for hh in heads:
            cols = slice(hh * hd, (hh + 1) * hd)
            for direction, c in ((0, i), (1, n_chunks - 1 - i)):
                rows = chunk_rows(c)
                kt = kst[c, hh]
                v = v_ref[rows, cols]
                state = st[hh, direction]
                if need_out:
                    q = qs[rows, cols]
                    carried = jnp.dot(q, state.astype(BF16), preferred_element_type=F32)
                    if direction == 0:
                        s = jnp.dot(q, kt, preferred_element_type=F32)
                        o = jnp.dot((s * mask[hh]).astype(BF16), v, preferred_element_type=F32)
                        accf[rows, cols] = o + qdec_f[hh] * carried
                    else:
                        accb[rows, cols] = qdec_b[hh] * carried
                vdec, cdec = (vdec_f, cdec_f) if direction == 0 else (vdec_b, cdec_b)
                st[hh, direction] = state * cdec[hh] + jnp.dot(
                    kt, (v.astype(F32) * vdec[hh]).astype(BF16), preferred_element_type=F32)
        return carry

    lax.fori_loop(0, n_chunks, scan, 0, unroll=2)

    if need_out:
        def finish(c, carry):
            rows = chunk_rows(c)
            for hh in heads:
                cols = slice(hh * hd, (hh + 1) * hd)
                o = accf[rows, cols] + accb[rows, cols]
                mu = jnp.mean(o, axis=-1, keepdims=True)
                d = o - mu
                var = jnp.mean(d * d, axis=-1, keepdims=True)
                gate = _silu(g_ref[rows, cols].astype(F32))
                y_ref[rows, cols] = (gate * (d * lax.rsqrt(var + GN_EPS))).astype(BF16)
            return carry

        lax.fori_loop(0, n_chunks, finish, 0, unroll=2)

    for hh in heads:
        sfo_ref[0, hh] = st[hh, 0]
        sbo_ref[0, hh] = st[hh, 1]


def _retention(z, seg, layer, dec_all, rope, s_f, s_b, need_out, prev):
    bsz, seq, rb0 = seg
    hd = RET_HEAD_DIM
    hps = RET_HPS
    wide = hps * hd
    use_rope = rope is not None
    col = lambda off: pl.BlockSpec((seq, wide), lambda bi, h: (rb0 + bi, off // wide + h))
    state = pl.BlockSpec((1, hps, hd, hd), lambda bi, h: (bi, h, 0, 0))
    in_specs = [pl.BlockSpec((None, hps, 2, hd), lambda bi, h: (layer, h, 0, 0)),
                col(O_Q), col(O_K), col(O_V), col(O_G)]
    args = [dec_all, z, z, z, z]
    if use_rope:
        in_specs += [pl.BlockSpec((seq, hd), lambda bi, h: (0, 0))] * 2
        args += list(rope)
    in_specs += [state, state]
    args += [s_f, s_b]
    out_specs = [state, state]
    out_shape = [jax.ShapeDtypeStruct((bsz, RET_HEADS, hd, hd), F32)] * 2
    n_chunks = seq // RET_CHUNK
    scratch = [pltpu.VMEM((seq, wide), BF16), pltpu.VMEM((n_chunks, hps, hd, RET_CHUNK), BF16),
               pltpu.VMEM((seq, wide), F32), pltpu.VMEM((seq, wide), F32),
               pltpu.VMEM((hps, 2, hd, hd), F32)]
    kw = dict(grid=(bsz, RET_HEADS // hps), scratch_shapes=scratch,
              compiler_params=_params(("parallel", "parallel"), 56))
    kern = functools.partial(_ret_kernel, use_rope=use_rope, need_out=need_out)
    if not need_out:
        res = pl.pallas_call(kern, in_specs=in_specs, out_specs=out_specs, out_shape=out_shape,
                             name="retention_state", **kw)(*args)
        return None, res[0], res[1]
    out_specs = [pl.BlockSpec((seq, wide), lambda bi, h: (rb0 + bi, h))] + out_specs
    out_shape = [jax.ShapeDtypeStruct((z.shape[0], D_RET), BF16)] + out_shape
    res = _seg_call(kern, prev=prev, n_in=len(args), in_specs=in_specs, out_specs=out_specs,
                    out_shape=out_shape, name="retention", args=args, **kw)
    return res[0], res[1], res[2]


def _rope_tables(seq):
    rows = seq // GRID_W
    row = jnp.repeat(jnp.arange(rows, dtype=F32), GRID_W)
    colp = jnp.tile(jnp.arange(GRID_W, dtype=F32), rows)
    inv = ROPE_BASE ** (-jnp.arange(ROPE_PAIRS, dtype=F32) / ROPE_PAIRS)
    ang_r = row[:, None] * inv[None, :]
    ang_c = colp[:, None] * inv[None, :]
    cr, sr, cc, sc = jnp.cos(ang_r), jnp.sin(ang_r), jnp.cos(ang_c), jnp.sin(ang_c)
    return (jnp.concatenate([cr, cr, cc, cc], axis=-1),
            jnp.concatenate([-sr, sr, -sc, sc], axis=-1))


def _merge_kernel(ya_ref, yb_ref, yc_ref, ga_ref, gb_ref, gc_ref, pa_ref, pb_ref, pc_ref, o_ref):
    def branch(g_ref, y_ref, p_ref, rows):
        return (jax.nn.sigmoid(g_ref[rows, :].astype(F32))
                * jnp.dot(y_ref[rows, :], p_ref[0], preferred_element_type=F32))

    sub = o_ref.shape[0] // MERGE_SUB
    for s in range(MERGE_SUB):
        rows = slice(s * sub, (s + 1) * sub)
        m = (branch(ga_ref, ya_ref, pa_ref, rows) + branch(gb_ref, yb_ref, pb_ref, rows)
             + branch(gc_ref, yc_ref, pc_ref, rows))
        o_ref[rows, :] = m.astype(BF16)


def _merge(ya, yb, yc, z, layer, p_a, p_b, p_c, m_rows):
    tm, tn = TM_WS, TN_WS
    goff = O_GATE // tn
    gstep = D_MODEL // tn
    act = lambda width: pl.BlockSpec((tm, width), lambda i, j: (i, 0))
    gate = lambda br: pl.BlockSpec((tm, tn), lambda i, j: (i, goff + br * gstep + j))
    wgt = lambda rows: pl.BlockSpec((1, rows, tn), lambda i, j: (layer, 0, j))
    return pl.pallas_call(
        _merge_kernel,
        grid=(m_rows // tm, D_MODEL // tn),
        in_specs=[act(D_POOL), act(D_HYENA), act(D_RET), gate(0), gate(1), gate(2),
                  wgt(D_POOL), wgt(D_HYENA), wgt(D_RET)],
        out_specs=pl.BlockSpec((tm, tn), lambda i, j: (i, j)),
        out_shape=jax.ShapeDtypeStruct((z.shape[0], D_MODEL), BF16),
        compiler_params=_params(("parallel", "arbitrary"), 48),
        name="merge",
    )(ya, yb, yc, z, z, z, p_a, p_b, p_c)


def _res_ln_kernel(*refs, emit_xm, n_k):
    a_ref, w_ref, b_ref, x_ref, gt_ref, g_ref, be_ref = refs[:7]
    refs = refs[7:]
    if emit_xm:
        sh_ref, sc_ref, xo_ref, xm_ref, acc_ref = refs
    else:
        xo_ref, acc_ref = refs
    k = pl.program_id(1)
    sub = x_ref.shape[0] // RES_SUB

    def finish(r0):
        rr = slice(r0, r0 + LN_ROWS)
        r = DEEPNORM_ALPHA * x_ref[rr, :] + gt_ref[0] * (acc_ref[rr, :] + b_ref[0])
        xn = _layer_norm(r, g_ref[0], be_ref[0])
        xo_ref[rr, :] = xn
        if emit_xm:
            xm_ref[rr, :] = (xn * (1.0 + sc_ref[0]) + sh_ref[0]).astype(BF16)

    def step(first, last):
        for s in range(RES_SUB):
            rows = slice(s * sub, (s + 1) * sub)
            part = jnp.dot(a_ref[rows, :], w_ref[0], preferred_element_type=F32)
            if first:
                acc_ref[rows, :] = part
            else:
                acc_ref[rows, :] += part
            if last:
                for r0 in range(s * sub, (s + 1) * sub, LN_ROWS):
                    finish(r0)

    if n_k == 1:
        step(True, True)
    else:
        pl.when(k == 0)(lambda: step(True, False))
        pl.when(jnp.logical_and(k > 0, k < n_k - 1))(lambda: step(False, False))
        pl.when(k == n_k - 1)(lambda: step(False, True))


def _res_ln(a, w_all, w_layer, b_all, x, mod, layer, gate_blk, ln_g, ln_b, next_mod, m_rows, seq, bsz, name):
    kdim = a.shape[1]
    tm = TM_ROW
    tk = min(TK_ROW, kdim)
    per = seq // tm
    gidx = _mod_index(layer, per, bsz)
    vec = lambda arr: arr.reshape(DEPTH, 1, D_MODEL)
    lvec = pl.BlockSpec((1, 1, D_MODEL), lambda i, k: (layer, 0, 0))
    in_specs = [pl.BlockSpec((tm, tk), lambda i, k: (i, k)),
                pl.BlockSpec((1, tk, D_MODEL), lambda i, k: (w_layer, k, 0)),
                lvec,
                pl.BlockSpec((tm, D_MODEL), lambda i, k: (i, 0)),
                pl.BlockSpec((1, 1, D_MODEL), lambda i, k: (gidx(i), 0, gate_blk)),
                lvec, lvec]
    args = [a, w_all, vec(b_all), x, mod, vec(ln_g), vec(ln_b)]
    row_out = pl.BlockSpec((tm, D_MODEL), lambda i, k: (i, 0))
    out_specs = [row_out]
    out_shape = [jax.ShapeDtypeStruct((m_rows, D_MODEL), F32)]
    if next_mod is not None:
        nl, sh_blk, sc_blk = next_mod
        nidx = _mod_index(nl, per, bsz)
        in_specs += [pl.BlockSpec((1, 1, D_MODEL), lambda i, k: (nidx(i), 0, sh_blk)),
                     pl.BlockSpec((1, 1, D_MODEL), lambda i, k: (nidx(i), 0, sc_blk))]
        args += [mod, mod]
        out_specs.append(row_out)
        out_shape.append(jax.ShapeDtypeStruct((m_rows, D_MODEL), BF16))
    res = pl.pallas_call(
        functools.partial(_res_ln_kernel, emit_xm=next_mod is not None, n_k=kdim // tk),
        grid=(m_rows // tm, kdim // tk),
        in_specs=in_specs,
        out_specs=out_specs,
        out_shape=out_shape,
        scratch_shapes=[pltpu.VMEM((tm, D_MODEL), F32)],
        compiler_params=_params(("parallel", "arbitrary"), 56),
        name=name,
    )(*args)
    return (res[0], res[1]) if next_mod is not None else (res[0], None)


def kernel(x, c, ctx, c_ctx, w_ada, b_ada, w_in, b_in, conv_w, conv_b, pool_w, pool_scale, filt_w1, filt_b1, filt_f1, filt_w2, filt_b2, filt_f2, filt_w3, filt_b3, filt_f3, filt_w4, hyena_d, ret_decay, p_a, p_b, p_c, w_o, b_o, ln1_g, ln1_b, w_mlp1, b_mlp1, w_mlp2, b_mlp2, ln2_g, ln2_b):
    bsz, seq, _ = x.shape
    ctx_len = ctx.shape[1]
    assert x.shape == (bsz, seq, D_MODEL) and ctx.shape == (bsz, ctx_len, D_MODEL)
    assert seq % RET_CHUNK == 0 and ctx_len % RET_CHUNK == 0 and seq % GRID_W == 0
    assert bsz + 1 <= ADA_ROWS and seq % ctx_len == 0
    n_lat, n_ctx = bsz * seq, bsz * ctx_len
    m_total = n_lat + n_ctx
    assert seq % TM_WS == 0 and n_ctx % TM_WS == 0 and seq % TM_ROW == 0 and n_ctx % TM_ROW == 0
    seg_x = (bsz, seq, 0)
    seg_c = (bsz, ctx_len, n_lat // ctx_len)

    cvec = jnp.concatenate([c, c_ctx[None, :], jnp.zeros((ADA_ROWS - bsz - 1, D_MODEL), F32)], axis=0)
    mod = _ada(cvec, w_ada, b_ada).reshape(DEPTH * ADA_ROWS, 1, 6 * D_MODEL)

    deltas = _filter_deltas()
    fwd_x, inv_x = _dft_matrices(seq)
    fwd_c, inv_c = _dft_matrices(ctx_len)
    zfeat_x, zfeat_c = _filter_features(seq), _filter_features(ctx_len)
    rope = _rope_tables(seq)
    zero_state = jnp.zeros((bsz, RET_HEADS, RET_HEAD_DIM, RET_HEAD_DIM), F32)

    row3 = lambda a: a.reshape(DEPTH, 1, -1)
    fp = {'w1': jnp.pad(filt_w1, ((0, 0), (0, FEAT_PAD - FILTER_EMB), (0, 0))),
          'b1': row3(filt_b1), 'f1': row3(filt_f1), 'w2': filt_w2, 'b2': row3(filt_b2), 'f2': row3(filt_f2),
          'w3': filt_w3, 'b3': row3(filt_b3), 'f3': row3(filt_f3), 'w4': filt_w4}
    dec_all = jnp.broadcast_to(jnp.swapaxes(ret_decay, 1, 2)[:, :, :, None],
                               (DEPTH, RET_HEADS, 2, RET_HEAD_DIM))
    pa_b, pb_b, pc_b = p_a.astype(BF16), p_b.astype(BF16), p_c.astype(BF16)
    wo_b = w_o.astype(BF16)

    xs, xm = _mod0(x.reshape(n_lat, D_MODEL), ctx.reshape(n_ctx, D_MODEL), mod, seq, bsz)

    def hyena(z, seg, l, fwd, inv, zfeat, prev):
        uu, x0 = _hy_pre(z, seg, l, conv_w, conv_b)
        kspec = _filter_spectrum(fwd, _filters(zfeat, deltas, l, fp))
        return _hy_inv(inv, _hy_fwd(fwd, uu, kspec, l, hyena_d), x0, seg, m_total, prev)

    for l in range(DEPTH):
        last = l == DEPTH - 1
        rows = n_lat if last else m_total
        z = _wsmm(xm, w_in, b_in, l, m_total, False, "in_proj")
        if not last:
            ya = _pool(z, seg_c, l, pool_w, pool_scale, None)
            yb = hyena(z, seg_c, l, fwd_c, inv_c, zfeat_c, None)
            yc, s_f, s_b = _retention(z, seg_c, l, dec_all, None, zero_state, zero_state, True, None)
        else:
            ya = yb = None
            yc, s_f, s_b = _retention(z, seg_c, l, dec_all, None, zero_state, zero_state, False, None)
        ya = _pool(z, seg_x, l, pool_w, pool_scale, ya)
        yb = hyena(z, seg_x, l, fwd_x, inv_x, zfeat_x, yb)
        yc, _, _ = _retention(z, seg_x, l, dec_all, rope, s_f, s_b, True, yc)
        merged = _merge(ya, yb, yc, z, l, pa_b, pb_b, pc_b, rows)
        xs, xm = _res_ln(merged, wo_b, l, b_o, xs, mod, l, 2, ln1_g, ln1_b, (l, 3, 4), rows, seq, bsz,
                         "out_proj_ln1")
        hid, w2_b = _wsmm(xm, w_mlp1, b_mlp1, l, rows, True, "mlp_up", side=w_mlp2)
        xs, xm = _res_ln(hid, w2_b[None], 0, b_mlp2, xs, mod, l, 5, ln2_g, ln2_b,
                         None if last else (l + 1, 0, 1), rows, seq, bsz, "mlp_down_ln2")
    return xs.reshape(bsz, seq, D_MODEL)
```

```python
import functools
import math

import jax
import jax.numpy as jnp
from jax import lax
from jax.experimental import pallas as pl
from jax.experimental.pallas import tpu as pltpu

F32 = jnp.float32
BF16 = jnp.bfloat16

D_MODEL = 2048
DEPTH = 4
GRID_W = 64
D_POOL = D_MODEL // 4
POOL_WINDOWS = (2, 4, 8, 16)
POOL_GROUP = D_POOL // len(POOL_WINDOWS)
D_HYENA = D_MODEL // 4
FILTER_EMB = 33
FILTER_BANDS = (FILTER_EMB - 1) // 2
FILTER_ORDER = 64
FILTER_DECAY_TARGET = 1e-2
FILTER_FAST_PCT = 0.3
FILTER_SLOW_PCT = 1.5
RET_HEAD_DIM = 256
D_RET = D_MODEL // 2
RET_HEADS = D_RET // RET_HEAD_DIM
RET_CHUNK = 128
ROPE_BASE = 10000.0
ROPE_PAIRS = RET_HEAD_DIM // 4
N_BRANCH = 3
D_FF = 4 * D_MODEL
LN_EPS = 1e-5
GN_EPS = 1e-6
DEEPNORM_ALPHA = (2 * DEPTH) ** 0.25
O_POOL = 0
O_HY = O_POOL + D_POOL
O_Q = O_HY + 3 * D_HYENA
O_K = O_Q + D_RET
O_V = O_K + D_RET
O_G = O_V + D_RET
O_GATE = O_G + D_RET
D_IN = O_GATE + N_BRANCH * D_MODEL

LANE = 128
FEAT_PAD = LANE
POOL_PAD = 16
ADA_ROWS = 8
MIB = 1024 * 1024

TM_WS = 1024
TN_WS = 1024
WS_SLABS = 8
TM_ROW = 512
TK_ROW = 2048
RES_SUB = 4
MERGE_SUB = 2
LN_ROWS = 16
HY_CT = 256
DFT_RADIX = 64
RET_HPS = 2


def _params(semantics, vmem_mib):
    return pltpu.CompilerParams(dimension_semantics=semantics, vmem_limit_bytes=vmem_mib * MIB)


def _silu(v):
    return v * jax.nn.sigmoid(v)


def _layer_norm(r, g, b):
    mu = jnp.mean(r, axis=-1, keepdims=True)
    d = r - mu
    var = jnp.mean(d * d, axis=-1, keepdims=True)
    return d * lax.rsqrt(var + LN_EPS) * g + b


def _skip_ref(kern, idx):
    def wrapped(*refs):
        return kern(*refs[:idx], *refs[idx + 1:])
    return wrapped


def _seg_call(kern, *, prev, n_in, **kw):
    in_specs = list(kw.pop("in_specs")) + [pl.BlockSpec(memory_space=pl.ANY)]
    args = list(kw.pop("args")) + [prev]
    return pl.pallas_call(_skip_ref(kern, n_in), in_specs=in_specs,
                          input_output_aliases={n_in: 0}, **kw)(*args)


def _mod_index(layer, per_batch_tiles, bsz):
    return lambda i: layer * ADA_ROWS + jnp.minimum(i // per_batch_tiles, bsz)


def _ada_kernel(c_ref, w_ref, b_ref, o_ref):
    s = _silu(c_ref[...]).astype(BF16)
    o_ref[0] = jnp.dot(s, w_ref[0].astype(BF16), preferred_element_type=F32) + b_ref[0]


def _ada(cvec, w_ada, b_ada):
    tn = 1024
    n = w_ada.shape[2]
    return pl.pallas_call(
        _ada_kernel,
        grid=(DEPTH, n // tn),
        in_specs=[
            pl.BlockSpec((ADA_ROWS, D_MODEL), lambda l, j: (0, 0)),
            pl.BlockSpec((1, D_MODEL, tn), lambda l, j: (l, 0, j)),
            pl.BlockSpec((1, 1, tn), lambda l, j: (l, 0, j)),
        ],
        out_specs=pl.BlockSpec((1, ADA_ROWS, tn), lambda l, j: (l, 0, j)),
        out_shape=jax.ShapeDtypeStruct((DEPTH, ADA_ROWS, n), F32),
        compiler_params=_params(("parallel", "parallel"), 40),
        name="ada",
    )(cvec, w_ada, b_ada.reshape(DEPTH, 1, n))


def _mod0_kernel(x_ref, c_ref, sh_ref, sc_ref, xo_ref, xm_ref, *, n_lat_tiles):
    def emit(v):
        xo_ref[...] = v
        xm_ref[...] = (v * (1.0 + sc_ref[0]) + sh_ref[0]).astype(BF16)

    i = pl.program_id(0)

    @pl.when(i < n_lat_tiles)
    def _():
        emit(x_ref[...])

    @pl.when(i >= n_lat_tiles)
    def _():
        emit(c_ref[...])


def _mod0(x2d, c2d, mod, seq, bsz):
    n_lat, n_ctx = x2d.shape[0], c2d.shape[0]
    tm = TM_ROW
    nl = n_lat // tm
    midx = _mod_index(0, seq // tm, bsz)
    return pl.pallas_call(
        functools.partial(_mod0_kernel, n_lat_tiles=nl),
        grid=((n_lat + n_ctx) // tm,),
        in_specs=[pl.BlockSpec((tm, D_MODEL), lambda i: (jnp.minimum(i, nl - 1), 0)),
                  pl.BlockSpec((tm, D_MODEL), lambda i: (jnp.maximum(i - nl, 0), 0)),
                  pl.BlockSpec((1, 1, D_MODEL), lambda i: (midx(i), 0, 0)),
                  pl.BlockSpec((1, 1, D_MODEL), lambda i: (midx(i), 0, 1))],
        out_specs=[pl.BlockSpec((tm, D_MODEL), lambda i: (i, 0)),
                   pl.BlockSpec((tm, D_MODEL), lambda i: (i, 0))],
        out_shape=[jax.ShapeDtypeStruct((n_lat + n_ctx, D_MODEL), F32),
                   jax.ShapeDtypeStruct((n_lat + n_ctx, D_MODEL), BF16)],
        compiler_params=_params(("parallel",), 40),
        name="assemble_modulate",
    )(x2d, c2d, mod, mod)


def _regroup(a, b):
    lo = lax.broadcasted_iota(jnp.int32, a.shape, 1) < LANE // 2
    return (jnp.where(lo, a, pltpu.roll(b, LANE // 2, 1)),
            jnp.where(lo, pltpu.roll(a, LANE // 2, 1), b))


def _wsmm_kernel(*refs, sq_relu, n_side, regroup_tiles):
    x_ref, w_ref, b_ref = refs[:3]
    s_refs = refs[3:3 + n_side]
    o_ref = refs[3 + n_side]
    so_refs = refs[4 + n_side:4 + 2 * n_side]
    wb_ref, bb_ref = refs[4 + 2 * n_side:]
    for s_ref, so_ref in zip(s_refs, so_refs):
        so_ref[...] = s_ref[0].astype(BF16)
    j = pl.program_id(0)
    first = pl.program_id(1) == 0
    tn = wb_ref.shape[1]

    def plain():
        wb_ref[...] = w_ref[0].astype(BF16)
        bb_ref[...] = jnp.broadcast_to(b_ref[0], bb_ref.shape)

    def regrouped():
        b8 = jnp.broadcast_to(b_ref[0], bb_ref.shape)
        for c0 in range(0, tn, 2 * LANE):
            lo, hi = slice(c0, c0 + LANE), slice(c0 + LANE, c0 + 2 * LANE)
            wa, wc = _regroup(w_ref[0, :, lo], w_ref[0, :, hi])
            wb_ref[:, lo] = wa.astype(BF16)
            wb_ref[:, hi] = wc.astype(BF16)
            ba, bc = _regroup(b8[:, lo], b8[:, hi])
            bb_ref[:, lo] = ba
            bb_ref[:, hi] = bc

    if regroup_tiles:
        hit = functools.reduce(jnp.logical_or, [j == t for t in regroup_tiles])
        pl.when(jnp.logical_and(first, hit))(regrouped)
        pl.when(jnp.logical_and(first, jnp.logical_not(hit)))(plain)
    else:
        pl.when(first)(plain)

    y = jnp.dot(x_ref[...], wb_ref[...], preferred_element_type=F32) + bb_ref[0:1, :]
    if sq_relu:
        y = jnp.square(jnp.maximum(y, 0.0))
    o_ref[...] = y.astype(BF16)


def _wsmm(xm, w_all, b_all, layer, m_rows, sq_relu, name, sides=(), regroup_tiles=()):
    kdim, n = w_all.shape[1], w_all.shape[2]
    tm, tn = TM_WS, TN_WS
    n_j, n_i = n // tn, m_rows // tm
    assert n_i >= WS_SLABS
    in_specs = [pl.BlockSpec((tm, kdim), lambda j, i: (i, 0)),
                pl.BlockSpec((1, kdim, tn), lambda j, i: (layer, 0, j)),
                pl.BlockSpec((1, 1, tn), lambda j, i: (layer, 0, j))]
    args = [xm, w_all, b_all.reshape(DEPTH, 1, n)]
    out_specs = [pl.BlockSpec((tm, tn), lambda j, i: (i, j))]
    out_shape = [jax.ShapeDtypeStruct((xm.shape[0], n), BF16)]
    for side in sides:
        r, c = side.shape[1:]
        rows = 16
        while r % (rows * WS_SLABS) or r // (rows * WS_SLABS) > n_j:
            rows *= 2
            assert rows * WS_SLABS <= r
        n_outer = r // (rows * WS_SLABS)
        sidx = functools.partial(
            lambda j, i, n_outer: jnp.minimum(j, n_outer - 1) * WS_SLABS + jnp.minimum(i, WS_SLABS - 1),
            n_outer=n_outer)
        in_specs.append(pl.BlockSpec((1, rows, c), functools.partial(
            lambda j, i, sidx: (layer, sidx(j, i), 0), sidx=sidx)))
        args.append(side)
        out_specs.append(pl.BlockSpec((rows, c), functools.partial(
            lambda j, i, sidx: (sidx(j, i), 0), sidx=sidx)))
        out_shape.append(jax.ShapeDtypeStruct((r, c), BF16))
    return pl.pallas_call(
        functools.partial(_wsmm_kernel, sq_relu=sq_relu, n_side=len(sides), regroup_tiles=regroup_tiles),
        grid=(n_j, n_i),
        in_specs=in_specs,
        out_specs=out_specs,
        out_shape=out_shape,
        scratch_shapes=[pltpu.VMEM((kdim, tn), BF16), pltpu.VMEM((8, tn), F32)],
        compiler_params=_params(("parallel", "arbitrary"), 48),
        name=name,
    )(*args)


def _pool_kernel(z_ref, w_ref, s_ref, o_ref, pad_ref):
    seq = z_ref.shape[0]
    rows = seq + 2 * POOL_PAD
    zeros = jnp.zeros((POOL_PAD, POOL_GROUP), F32)
    pad_ref[0:POOL_PAD, :] = zeros
    pad_ref[POOL_PAD + seq:rows, :] = zeros
    t = lax.broadcasted_iota(jnp.int32, (seq, POOL_GROUP), 0)
    for g, win in enumerate(POOL_WINDOWS):
        cols = slice(g * POOL_GROUP, (g + 1) * POOL_GROUP)
        u = z_ref[:, cols].astype(F32)
        pad_ref[POOL_PAD:POOL_PAD + seq, :] = u
        w = pad_ref[...]
        w = pltpu.roll(w, 1, 0) + w
        width = 2
        while width < win:
            half = width // 2
            w = pltpu.roll(w, half, 0) + pltpu.roll(w, rows - half, 0)
            width *= 2
        half = win // 2
        count = jnp.minimum(t + half, seq) - jnp.maximum(t - half, 0)
        pooled = w[POOL_PAD:POOL_PAD + seq, :] / count.astype(F32) - u
        y = jnp.dot(pooled.astype(BF16), w_ref[0, g].astype(BF16), preferred_element_type=F32)
        o_ref[:, cols] = (y * s_ref[0, :, cols]).astype(BF16)


def _pool(z, seg, layer, pool_w, pool_scale, prev):
    bsz, seq, rb0 = seg
    ng = len(POOL_WINDOWS)
    return _seg_call(
        _pool_kernel, prev=prev, n_in=3,
        grid=(bsz,),
        in_specs=[pl.BlockSpec((seq, D_POOL), lambda bi: (rb0 + bi, O_POOL // D_POOL)),
                  pl.BlockSpec((1, ng, POOL_GROUP, POOL_GROUP), lambda bi: (layer, 0, 0, 0)),
                  pl.BlockSpec((1, 1, D_POOL), lambda bi: (layer, 0, 0))],
        out_specs=pl.BlockSpec((seq, D_POOL), lambda bi: (rb0 + bi, 0)),
        out_shape=jax.ShapeDtypeStruct((z.shape[0], D_POOL), BF16),
        scratch_shapes=[pltpu.VMEM((seq + 2 * POOL_PAD, POOL_GROUP), F32)],
        compiler_params=_params(("parallel",), 48),
        name="pool",
        args=[z, pool_w, pool_scale.reshape(DEPTH, 1, D_POOL)],
    )


def _conv3(u, w, b):
    seq = u.shape[0]
    t = lax.broadcasted_iota(jnp.int32, u.shape, 0)
    prev = jnp.where(t == 0, 0.0, pltpu.roll(u, 1, 0))
    nxt = jnp.where(t == seq - 1, 0.0, pltpu.roll(u, seq - 1, 0))
    return prev * w[0:1, :] + u * w[1:2, :] + nxt * w[2:3, :] + b


def _hy_pre_kernel(zv_ref, z0_ref, z1_ref, wv_ref, w0_ref, w1_ref, bv_ref, b0_ref, b1_ref,
                   uu_ref, x0_ref):
    v = _conv3(zv_ref[...].astype(F32), wv_ref[0], bv_ref[0])
    x1 = _conv3(z1_ref[...].astype(F32), w1_ref[0], b1_ref[0])
    uu_ref[0] = (v * x1).astype(BF16)
    x0_ref[0] = _conv3(z0_ref[...].astype(F32), w0_ref[0], b0_ref[0]).astype(BF16)


def _hy_pre(z, seg, layer, conv_w, conv_b):
    bsz, seq, rb0 = seg
    ct = HY_CT
    nct = D_HYENA // ct
    zoff = O_HY // ct
    zspec = [pl.BlockSpec((seq, ct), functools.partial(lambda bi, j, s: (rb0 + bi, zoff + s * nct + j), s=s))
             for s in range(3)]
    wspec = [pl.BlockSpec((1, 3, ct), functools.partial(lambda bi, j, s: (layer, 0, s * nct + j), s=s))
             for s in range(3)]
    bspec = [pl.BlockSpec((1, 1, ct), functools.partial(lambda bi, j, s: (layer, 0, s * nct + j), s=s))
             for s in range(3)]
    cb = conv_b.reshape(DEPTH, 1, 3 * D_HYENA)
    out = pl.BlockSpec((1, seq, ct), lambda bi, j: (bi, 0, j))
    return pl.pallas_call(
        _hy_pre_kernel,
        grid=(bsz, nct),
        in_specs=zspec + wspec + bspec,
        out_specs=[out, out],
        out_shape=[jax.ShapeDtypeStruct((bsz, seq, D_HYENA), BF16)] * 2,
        compiler_params=_params(("parallel", "parallel"), 48),
        name="hy_pre",
    )(z, z, z, conv_w, conv_w, conv_w, cb, cb, cb)


def _filt_kernel(zf_ref, w1_ref, b1_ref, f1_ref, w2_ref, b2_ref, f2_ref, w3_ref, b3_ref, f3_ref,
                 w4_ref, dl_ref, o_ref):
    tl = zf_ref.shape[0]
    zf = zf_ref[...]

    def dense(a, w_ref):
        return jnp.dot(a.astype(BF16), w_ref[0].astype(BF16), preferred_element_type=F32)

    hdn = jnp.sin(f1_ref[0] * (dense(zf, w1_ref) + b1_ref[0]))
    hdn = jnp.sin(f2_ref[0] * (dense(hdn, w2_ref) + b2_ref[0]))
    hdn = jnp.sin(f3_ref[0] * (dense(hdn, w3_ref) + b3_ref[0]))
    h = dense(hdn, w4_ref)
    decay = jnp.exp(-zf[:, 0:1] * jnp.abs(dl_ref[...]))
    row = lax.broadcasted_iota(jnp.int32, (tl, D_HYENA), 0) + pl.program_id(0) * tl
    o_ref[:, 0:D_HYENA] = (h[:, 0:D_HYENA] * decay).astype(BF16)
    o_ref[:, D_HYENA:] = jnp.where(row == 0, 0.0, h[:, D_HYENA:] * decay).astype(BF16)


def _filter_features(seq):
    t = jnp.linspace(0.0, 1.0, seq, dtype=F32)[:, None]
    w = 2.0 * math.pi * jnp.arange(seq, dtype=F32)[:, None] / seq
    f = jnp.linspace(1e-4, FILTER_BANDS - 1, FILTER_BANDS, dtype=F32)[None, :]
    z = jnp.concatenate([t, jnp.cos(f * w), -jnp.sin(f * w)], axis=-1)
    return jnp.pad(z, ((0, 0), (0, FEAT_PAD - FILTER_EMB)))


def _filter_deltas():
    max_decay = math.log(FILTER_DECAY_TARGET) / FILTER_FAST_PCT
    min_decay = math.log(FILTER_DECAY_TARGET) / FILTER_SLOW_PCT
    return jnp.linspace(min_decay, max_decay, D_HYENA, dtype=F32)[None, :]


def _filters(zfeat, deltas, layer, fp):
    seq = zfeat.shape[0]
    tl = min(256, seq)
    lsel = lambda shape: pl.BlockSpec((1,) + shape, lambda i: (layer,) + (0,) * len(shape))
    vec = lsel((1, FILTER_ORDER))
    sq = lsel((FILTER_ORDER, FILTER_ORDER))
    return pl.pallas_call(
        _filt_kernel,
        grid=(seq // tl,),
        in_specs=[pl.BlockSpec((tl, FEAT_PAD), lambda i: (i, 0)),
                  lsel((FEAT_PAD, FILTER_ORDER)), vec, vec, sq, vec, vec, sq, vec, vec,
                  lsel((FILTER_ORDER, 2 * D_HYENA)),
                  pl.BlockSpec((1, D_HYENA), lambda i: (0, 0))],
        out_specs=pl.BlockSpec((tl, 2 * D_HYENA), lambda i: (i, 0)),
        out_shape=jax.ShapeDtypeStruct((seq, 2 * D_HYENA), BF16),
        compiler_params=_params(("parallel",), 32),
        name="hy_filter",
    )(zfeat, fp['w1'], fp['b1'], fp['f1'], fp['w2'], fp['b2'], fp['f2'], fp['w3'], fp['b3'], fp['f3'],
      fp['w4'], deltas)


def _dft_kernel(ca_ref, sa_ref, cb_ref, sb_ref, fwd_ref, inv_ref):
    ca, sa = ca_ref[0], sa_ref[0]
    cb, sb = cb_ref[...], sb_ref[...]
    cosb = ca * cb - sa * sb
    sinb = sa * cb + ca * sb
    row = lax.broadcasted_iota(jnp.int32, cosb.shape, 0) + pl.program_id(0) * cosb.shape[0]
    col = lax.broadcasted_iota(jnp.int32, cosb.shape, 1)
    alt = lambda idx: jnp.where(jnp.bitwise_and(idx, 1) == 0, 1.0, -1.0)
    fwd_ref[0] = cosb.astype(BF16)
    fwd_ref[1] = jnp.where(row == 0, alt(col), sinb).astype(BF16)
    inv_ref[...] = jnp.where(col == 0, alt(row), sinb).astype(BF16)


def _dft_matrices(seq):
    n = 2 * seq
    radix = DFT_RADIX
    t = jnp.arange(seq, dtype=jnp.int32)[None, :]
    ang = lambda f: ((f * t) % n).astype(F32) * (2.0 * math.pi / n)
    ang_a = ang(radix * jnp.arange(seq // radix, dtype=jnp.int32)[:, None])[:, None, :]
    ang_b = ang(jnp.arange(radix, dtype=jnp.int32)[:, None])
    coarse = pl.BlockSpec((1, 1, seq), lambda i: (i, 0, 0))
    fine = pl.BlockSpec((radix, seq), lambda i: (0, 0))
    return pl.pallas_call(
        _dft_kernel,
        grid=(seq // radix,),
        in_specs=[coarse, coarse, fine, fine],
        out_specs=[pl.BlockSpec((2, radix, seq), lambda i: (0, i, 0)),
                   pl.BlockSpec((radix, seq), lambda i: (i, 0))],
        out_shape=[jax.ShapeDtypeStruct((2, seq, seq), BF16), jax.ShapeDtypeStruct((seq, seq), BF16)],
        compiler_params=_params(("parallel",), 32),
        name="dft_tables",
    )(jnp.cos(ang_a), jnp.sin(ang_a), jnp.cos(ang_b), jnp.sin(ang_b))


def _mm_kernel(a_ref, b_ref, o_ref):
    o_ref[0] = jnp.dot(a_ref[0], b_ref[...], preferred_element_type=F32)


def _filter_spectrum(fwd, hcat):
    _, seq, _ = fwd.shape
    n = hcat.shape[1]
    tm = min(512, seq)
    tn = 512
    return pl.pallas_call(
        _mm_kernel,
        grid=(2, seq // tm, n // tn),
        in_specs=[pl.BlockSpec((1, tm, seq), lambda h, i, j: (h, i, 0)),
                  pl.BlockSpec((seq, tn), lambda h, i, j: (0, j))],
        out_specs=pl.BlockSpec((1, tm, tn), lambda h, i, j: (h, i, j)),
        out_shape=jax.ShapeDtypeStruct((2, seq, n), F32),
        compiler_params=_params(("parallel", "parallel", "parallel"), 32),
        name="hy_filter_dft",
    )(fwd, hcat)


def _hy_fwd_kernel(f_ref, uu_ref, kf_ref, kb_ref, d_ref, y_ref, *, n_fft):
    tm = f_ref.shape[1]
    uu = uu_ref[0]
    a = jnp.dot(f_ref[0], uu, preferred_element_type=F32)
    b = jnp.dot(f_ref[1], uu, preferred_element_type=F32)
    row0 = (lax.broadcasted_iota(jnp.int32, a.shape, 0) + pl.program_id(0) * tm) == 0
    ka = kf_ref[0] + kb_ref[0] + d_ref[0]
    kb_sum = kf_ref[1] + kb_ref[1] + d_ref[0]
    kb_dif = kf_ref[1] - kb_ref[1]
    ya = jnp.where(row0, a * ka, a * ka - b * kb_dif)
    yb = jnp.where(row0, b * kb_sum, a * kb_dif + b * ka)
    wgt = jnp.where(row0, 1.0 / n_fft, 2.0 / n_fft)
    y_ref[0, 0] = (ya * wgt).astype(BF16)
    y_ref[0, 1] = (yb * wgt).astype(BF16)


def _hy_fwd(fwd, uu, kspec, layer, hyena_d):
    bsz, seq, _ = uu.shape
    ct = HY_CT
    nct = D_HYENA // ct
    tm = min(512, seq)
    return pl.pallas_call(
        functools.partial(_hy_fwd_kernel, n_fft=2 * seq),
        grid=(seq // tm, nct, bsz),
        in_specs=[pl.BlockSpec((2, tm, seq), lambda i, j, bi: (0, i, 0)),
                  pl.BlockSpec((1, seq, ct), lambda i, j, bi: (bi, 0, j)),
                  pl.BlockSpec((2, tm, ct), lambda i, j, bi: (0, i, j)),
                  pl.BlockSpec((2, tm, ct), lambda i, j, bi: (0, i, nct + j)),
                  pl.BlockSpec((1, 1, ct), lambda i, j, bi: (layer, 0, j))],
        out_specs=pl.BlockSpec((1, 2, tm, ct), lambda i, j, bi: (bi, 0, i, j)),
        out_shape=jax.ShapeDtypeStruct((bsz, 2, seq, D_HYENA), BF16),
        compiler_params=_params(("parallel", "parallel", "parallel"), 48),
        name="hy_fwd_dft",
    )(fwd, uu, kspec, kspec, hyena_d.reshape(DEPTH, 1, D_HYENA))


def _hy_inv_kernel(fc_ref, fs_ref, y_ref, x0_ref, o_ref):
    y = (jnp.dot(fc_ref[0], y_ref[0, 0], preferred_element_type=F32)
         + jnp.dot(fs_ref[...], y_ref[0, 1], preferred_element_type=F32))
    o_ref[...] = (y * x0_ref[0].astype(F32)).astype(BF16)


def _hy_inv(fwd, inv, yspec, x0, seg, m_total, prev):
    bsz, seq, rb0 = seg
    ct = HY_CT
    nct = D_HYENA // ct
    tm = min(512, seq)
    per = seq // tm
    return _seg_call(
        _hy_inv_kernel, prev=prev, n_in=4,
        grid=(per, nct, bsz),
        in_specs=[pl.BlockSpec((1, tm, seq), lambda i, j, bi: (0, i, 0)),
                  pl.BlockSpec((tm, seq), lambda i, j, bi: (i, 0)),
                  pl.BlockSpec((1, 2, seq, ct), lambda i, j, bi: (bi, 0, 0, j)),
                  pl.BlockSpec((1, tm, ct), lambda i, j, bi: (bi, i, j))],
        out_specs=pl.BlockSpec((tm, ct), lambda i, j, bi: ((rb0 + bi) * per + i, j)),
        out_shape=jax.ShapeDtypeStruct((m_total, D_HYENA), BF16),
        compiler_params=_params(("parallel", "parallel", "parallel"), 48),
        name="hy_inv_dft",
        args=[fwd, inv, yspec, x0],
    )


def _ret_kernel(*refs, use_rope, need_out):
    refs = list(refs)
    dec_ref, q_ref, k_ref, v_ref, g_ref = refs[:5]
    refs = refs[5:]
    if use_rope:
        cos_ref, sin_ref = refs[:2]
        refs = refs[2:]
    sf0_ref, sb0_ref = refs[:2]
    refs = refs[2:]
    if need_out:
        y_ref = refs[0]
        refs = refs[1:]
    sfo_ref, sbo_ref, qs, kst, accf, accb, st = refs

    seq = k_ref.shape[0]
    csz = RET_CHUNK
    n_chunks = seq // csz
    half = n_chunks // 2
    hd = RET_HEAD_DIM
    heads = range(RET_HPS)

    pos = lax.broadcasted_iota(jnp.int32, (csz, hd), 0).astype(F32)
    ii = lax.broadcasted_iota(jnp.int32, (csz, csz), 0)
    jj = lax.broadcasted_iota(jnp.int32, (csz, csz), 1)
    rel = (ii - jj).astype(F32)
    qdec_f, vdec_f, qdec_b, vdec_b, cdec_f, cdec_b, mask = [], [], [], [], [], [], []
    for hh in heads:
        lg = jnp.log1p(-jnp.exp(dec_ref[hh]))
        lgf = lg[0:1, :]
        lgb = lg[1:2, :]
        qdec_f.append(jnp.exp((pos + 1.0) * lgf))
        vdec_f.append(jnp.exp((csz - 1.0 - pos) * lgf))
        qdec_b.append(jnp.exp((csz - pos) * lgb))
        vdec_b.append(jnp.exp(pos * lgb))
        cdec_f.append(jnp.exp(csz * lgf))
        cdec_b.append(jnp.exp(csz * lgb))
        mask.append(jnp.where(rel >= 0, jnp.exp(jnp.maximum(rel, 0.0) * lgf[:, :csz]), 0.0)
                    + jnp.where(rel <= 0, jnp.exp(jnp.maximum(-rel, 0.0) * lgb[:, :csz]), 0.0))

    def chunk_rows(c):
        return pl.ds(pl.multiple_of(c * csz, csz), csz)

    def rope(x, rows):
        if not use_rope:
            return x
        a, b = x[:, :LANE], x[:, LANE:]
        cs, sn = cos_ref[rows, :], sin_ref[rows, :]
        return jnp.concatenate([a * cs - b * sn, b * cs + a * sn], axis=1)

    def prep(c, carry):
        rows = chunk_rows(c)
        for hh in heads:
            cols = slice(hh * hd, (hh + 1) * hd)
            kc = rope(k_ref[rows, cols].astype(F32) * (hd ** -0.5), rows)
            kst[c, hh] = kc.T.astype(BF16)
            if need_out:
                qs[rows, cols] = rope(q_ref[rows, cols].astype(F32), rows).astype(BF16)
        return carry

    lax.fori_loop(0, n_chunks, prep, 0, unroll=2)

    for hh in heads:
        st[hh, 0] = sf0_ref[0, hh]
        st[hh, 1] = sb0_ref[0, hh]

    def visit(hh, direction, c, finish):
        cols = slice(hh * hd, (hh + 1) * hd)
        rows = chunk_rows(c)
        kt = kst[c, hh]
        v = v_ref[rows, cols]
        state = st[hh, direction]
        if need_out:
            q = qs[rows, cols]
            carried = jnp.dot(q, state.astype(BF16), preferred_element_type=F32)
            if direction == 0:
                s = jnp.dot(q, kt, preferred_element_type=F32)
                o = jnp.dot((s * mask[hh]).astype(BF16), v, preferred_element_type=F32)
                o = o + qdec_f[hh] * carried
                mine, other = accf, accb
            else:
                o = qdec_b[hh] * carried
                mine, other = accb, accf
            if finish:
                o = o + other[rows, cols]
                mu = jnp.mean(o, axis=-1, keepdims=True)
                d = o - mu
                var = jnp.mean(d * d, axis=-1, keepdims=True)
                gate = _silu(g_ref[rows, cols].astype(F32))
                y_ref[rows, cols] = (gate * (d * lax.rsqrt(var + GN_EPS))).astype(BF16)
            else:
                mine[rows, cols] = o
        vdec, cdec = (vdec_f, cdec_f) if direction == 0 else (vdec_b, cdec_b)
        st[hh, direction] = state * cdec[hh] + jnp.dot(
            kt, (v.astype(F32) * vdec[hh]).astype(BF16), preferred_element_type=F32)

    def scan(finish):
        def body(i, carry):
            for hh in heads:
                visit(hh, 0, i, finish)
                visit(hh, 1, n_chunks - 1 - i, finish)
            return carry
        return body

    lax.fori_loop(0, half, scan(False), 0, unroll=min(2, half))
    lax.fori_loop(half, n_chunks, scan(need_out), 0, unroll=min(2, half))

    for hh in heads:
        sfo_ref[0, hh] = st[hh, 0]
        sbo_ref[0, hh] = st[hh, 1]


def _retention(z, seg, layer, dec_all, rope, s_f, s_b, need_out, prev):
    bsz, seq, rb0 = seg
    assert (seq // RET_CHUNK) % 2 == 0
    hd = RET_HEAD_DIM
    hps = RET_HPS
    wide = hps * hd
    use_rope = rope is not None
    col = lambda off: pl.BlockSpec((seq, wide), lambda bi, h: (rb0 + bi, off // wide + h))
    state = pl.BlockSpec((1, hps, hd, hd), lambda bi, h: (bi, h, 0, 0))
    in_specs = [pl.BlockSpec((None, hps, 2, hd), lambda bi, h: (layer, h, 0, 0)),
                col(O_Q), col(O_K), col(O_V), col(O_G)]
    args = [dec_all, z, z, z, z]
    if use_rope:
        in_specs += [pl.BlockSpec((seq, LANE), lambda bi, h: (0, 0))] * 2
        args += list(rope)
    in_specs += [state, state]
    args += [s_f, s_b]
    out_specs = [state, state]
    out_shape = [jax.ShapeDtypeStruct((bsz, RET_HEADS, hd, hd), F32)] * 2
    n_chunks = seq // RET_CHUNK
    scratch = [pltpu.VMEM((seq, wide), BF16), pltpu.VMEM((n_chunks, hps, hd, RET_CHUNK), BF16),
               pltpu.VMEM((seq, wide), F32), pltpu.VMEM((seq, wide), F32),
               pltpu.VMEM((hps, 2, hd, hd), F32)]
    kw = dict(grid=(bsz, RET_HEADS // hps), scratch_shapes=scratch,
              compiler_params=_params(("parallel", "parallel"), 56))
    kern = functools.partial(_ret_kernel, use_rope=use_rope, need_out=need_out)
    if not need_out:
        res = pl.pallas_call(kern, in_specs=in_specs, out_specs=out_specs, out_shape=out_shape,
                             name="retention_state", **kw)(*args)
        return None, res[0], res[1]
    out_specs = [pl.BlockSpec((seq, wide), lambda bi, h: (rb0 + bi, h))] + out_specs
    out_shape = [jax.ShapeDtypeStruct((z.shape[0], D_RET), BF16)] + out_shape
    res = _seg_call(kern, prev=prev, n_in=len(args), in_specs=in_specs, out_specs=out_specs,
                    out_shape=out_shape, name="retention", args=args, **kw)
    return res[0], res[1], res[2]


def _rope_tables(seq):
    rows = seq // GRID_W
    row = jnp.repeat(jnp.arange(rows, dtype=F32), GRID_W)
    colp = jnp.tile(jnp.arange(GRID_W, dtype=F32), rows)
    inv = ROPE_BASE ** (-jnp.arange(ROPE_PAIRS, dtype=F32) / ROPE_PAIRS)
    ang_r = row[:, None] * inv[None, :]
    ang_c = colp[:, None] * inv[None, :]
    return (jnp.concatenate([jnp.cos(ang_r), jnp.cos(ang_c)], axis=-1),
            jnp.concatenate([jnp.sin(ang_r), jnp.sin(ang_c)], axis=-1))


def _merge_kernel(ya_ref, yb_ref, yc_ref, ga_ref, gb_ref, gc_ref, pa_ref, pb_ref, pc_ref, o_ref):
    def branch(g_ref, y_ref, p_ref, rows):
        return (jax.nn.sigmoid(g_ref[rows, :].astype(F32))
                * jnp.dot(y_ref[rows, :], p_ref[...], preferred_element_type=F32))

    sub = o_ref.shape[0] // MERGE_SUB
    for s in range(MERGE_SUB):
        rows = slice(s * sub, (s + 1) * sub)
        m = (branch(ga_ref, ya_ref, pa_ref, rows) + branch(gb_ref, yb_ref, pb_ref, rows)
             + branch(gc_ref, yc_ref, pc_ref, rows))
        o_ref[rows, :] = m.astype(BF16)


def _merge(ya, yb, yc, z, p_a, p_b, p_c, m_rows):
    tm, tn = TM_WS, TN_WS
    goff = O_GATE // tn
    gstep = D_MODEL // tn
    act = lambda width: pl.BlockSpec((tm, width), lambda i, j: (i, 0))
    gate = lambda br: pl.BlockSpec((tm, tn), lambda i, j: (i, goff + br * gstep + j))
    wgt = lambda rows: pl.BlockSpec((rows, tn), lambda i, j: (0, j))
    return pl.pallas_call(
        _merge_kernel,
        grid=(m_rows // tm, D_MODEL // tn),
        in_specs=[act(D_POOL), act(D_HYENA), act(D_RET), gate(0), gate(1), gate(2),
                  wgt(D_POOL), wgt(D_HYENA), wgt(D_RET)],
        out_specs=pl.BlockSpec((tm, tn), lambda i, j: (i, j)),
        out_shape=jax.ShapeDtypeStruct((m_rows, D_MODEL), BF16),
        compiler_params=_params(("parallel", "arbitrary"), 48),
        name="merge",
    )(ya, yb, yc, z, z, z, p_a, p_b, p_c)


def _res_ln_kernel(*refs, emit_xm, n_k):
    a_ref, w_ref, b_ref, x_ref, gt_ref, g_ref, be_ref = refs[:7]
    refs = refs[7:]
    if emit_xm:
        sh_ref, sc_ref, xo_ref, xm_ref, acc_ref = refs
    else:
        xo_ref, acc_ref = refs
    k = pl.program_id(1)
    sub = x_ref.shape[0] // RES_SUB

    def finish(r0):
        rr = slice(r0, r0 + LN_ROWS)
        r = DEEPNORM_ALPHA * x_ref[rr, :] + gt_ref[0] * (acc_ref[rr, :] + b_ref[0])
        xn = _layer_norm(r, g_ref[0], be_ref[0])
        xo_ref[rr, :] = xn
        if emit_xm:
            xm_ref[rr, :] = (xn * (1.0 + sc_ref[0]) + sh_ref[0]).astype(BF16)

    def step(first, last):
        for s in range(RES_SUB):
            rows = slice(s * sub, (s + 1) * sub)
            part = jnp.dot(a_ref[rows, :], w_ref[...], preferred_element_type=F32)
            if first:
                acc_ref[rows, :] = part
            else:
                acc_ref[rows, :] += part
            if last:
                for r0 in range(s * sub, (s + 1) * sub, LN_ROWS):
                    finish(r0)

    if n_k == 1:
        step(True, True)
    else:
        pl.when(k == 0)(lambda: step(True, False))
        pl.when(jnp.logical_and(k > 0, k < n_k - 1))(lambda: step(False, False))
        pl.when(k == n_k - 1)(lambda: step(False, True))


def _res_ln(a, w, b_all, x, mod, layer, gate_blk, ln_g, ln_b, next_mod, m_rows, seq, bsz, name):
    kdim = a.shape[1]
    tm = TM_ROW
    tk = min(TK_ROW, kdim)
    per = seq // tm
    gidx = _mod_index(layer, per, bsz)
    vec = lambda arr: arr.reshape(DEPTH, 1, D_MODEL)
    lvec = pl.BlockSpec((1, 1, D_MODEL), lambda i, k: (layer, 0, 0))
    in_specs = [pl.BlockSpec((tm, tk), lambda i, k: (i, k)),
                pl.BlockSpec((tk, D_MODEL), lambda i, k: (k, 0)),
                lvec,
                pl.BlockSpec((tm, D_MODEL), lambda i, k: (i, 0)),
                pl.BlockSpec((1, 1, D_MODEL), lambda i, k: (gidx(i), 0, gate_blk)),
                lvec, lvec]
    args = [a, w, vec(b_all), x, mod, vec(ln_g), vec(ln_b)]
    row_out = pl.BlockSpec((tm, D_MODEL), lambda i, k: (i, 0))
    out_specs = [row_out]
    out_shape = [jax.ShapeDtypeStruct((m_rows, D_MODEL), F32)]
    if next_mod is not None:
        nl, sh_blk, sc_blk = next_mod
        nidx = _mod_index(nl, per, bsz)
        in_specs += [pl.BlockSpec((1, 1, D_MODEL), lambda i, k: (nidx(i), 0, sh_blk)),
                     pl.BlockSpec((1, 1, D_MODEL), lambda i, k: (nidx(i), 0, sc_blk))]
        args += [mod, mod]
        out_specs.append(row_out)
        out_shape.append(jax.ShapeDtypeStruct((m_rows, D_MODEL), BF16))
    res = pl.pallas_call(
        functools.partial(_res_ln_kernel, emit_xm=next_mod is not None, n_k=kdim // tk),
        grid=(m_rows // tm, kdim // tk),
        in_specs=in_specs,
        out_specs=out_specs,
        out_shape=out_shape,
        scratch_shapes=[pltpu.VMEM((tm, D_MODEL), F32)],
        compiler_params=_params(("parallel", "arbitrary"), 56),
        name=name,
    )(*args)
    return (res[0], res[1]) if next_mod is not None else (res[0], None)


def kernel(x, c, ctx, c_ctx, w_ada, b_ada, w_in, b_in, conv_w, conv_b, pool_w, pool_scale, filt_w1, filt_b1, filt_f1, filt_w2, filt_b2, filt_f2, filt_w3, filt_b3, filt_f3, filt_w4, hyena_d, ret_decay, p_a, p_b, p_c, w_o, b_o, ln1_g, ln1_b, w_mlp1, b_mlp1, w_mlp2, b_mlp2, ln2_g, ln2_b):
    bsz, seq, _ = x.shape
    ctx_len = ctx.shape[1]
    assert x.shape == (bsz, seq, D_MODEL) and ctx.shape == (bsz, ctx_len, D_MODEL)
    assert seq % RET_CHUNK == 0 and ctx_len % RET_CHUNK == 0 and seq % GRID_W == 0
    assert bsz + 1 <= ADA_ROWS and seq % ctx_len == 0
    n_lat, n_ctx = bsz * seq, bsz * ctx_len
    m_total = n_lat + n_ctx
    assert seq % TM_WS == 0 and n_ctx % TM_WS == 0 and seq % TM_ROW == 0 and n_ctx % TM_ROW == 0
    assert O_Q % TN_WS == 0 and O_V % TN_WS == 0
    seg_x = (bsz, seq, 0)
    seg_c = (bsz, ctx_len, n_lat // ctx_len)

    cvec = jnp.concatenate([c, c_ctx[None, :], jnp.zeros((ADA_ROWS - bsz - 1, D_MODEL), F32)], axis=0)
    mod = _ada(cvec, w_ada, b_ada).reshape(DEPTH * ADA_ROWS, 1, 6 * D_MODEL)

    deltas = _filter_deltas()
    dft_x, dft_c = _dft_matrices(seq), _dft_matrices(ctx_len)
    zfeat_x, zfeat_c = _filter_features(seq), _filter_features(ctx_len)
    rope = _rope_tables(seq)
    zero_state = jnp.zeros((bsz, RET_HEADS, RET_HEAD_DIM, RET_HEAD_DIM), F32)

    row3 = lambda a: a.reshape(DEPTH, 1, -1)
    fp = {'w1': jnp.pad(filt_w1, ((0, 0), (0, FEAT_PAD - FILTER_EMB), (0, 0))),
          'b1': row3(filt_b1), 'f1': row3(filt_f1), 'w2': filt_w2, 'b2': row3(filt_b2), 'f2': row3(filt_f2),
          'w3': filt_w3, 'b3': row3(filt_b3), 'f3': row3(filt_f3), 'w4': filt_w4}
    dec_all = jnp.broadcast_to(jnp.swapaxes(ret_decay, 1, 2)[:, :, :, None],
                               (DEPTH, RET_HEADS, 2, RET_HEAD_DIM))
    xs, xm = _mod0(x.reshape(n_lat, D_MODEL), ctx.reshape(n_ctx, D_MODEL), mod, seq, bsz)

    def hyena(z, seg, l, dft, zfeat, prev):
        fwd, inv = dft
        uu, x0 = _hy_pre(z, seg, l, conv_w, conv_b)
        kspec = _filter_spectrum(fwd, _filters(zfeat, deltas, l, fp))
        return _hy_inv(fwd, inv, _hy_fwd(fwd, uu, kspec, l, hyena_d), x0, seg, m_total, prev)

    ya = jnp.zeros((m_total, D_POOL), BF16)
    yb = jnp.zeros((m_total, D_HYENA), BF16)
    yc = jnp.zeros((m_total, D_RET), BF16)
    qk_tiles = tuple(range(O_Q // TN_WS, O_V // TN_WS))
    for l in range(DEPTH):
        last = l == DEPTH - 1
        rows = n_lat if last else m_total
        z, wo_b, pa_b, pb_b, pc_b = _wsmm(xm, w_in, b_in, l, m_total, False, "in_proj",
                                          sides=(w_o, p_a, p_b, p_c), regroup_tiles=qk_tiles)
        if not last:
            ya = _pool(z, seg_c, l, pool_w, pool_scale, ya)
            yb = hyena(z, seg_c, l, dft_c, zfeat_c, yb)
            yc, s_f, s_b = _retention(z, seg_c, l, dec_all, None, zero_state, zero_state, True, yc)
        else:
            _, s_f, s_b = _retention(z, seg_c, l, dec_all, None, zero_state, zero_state, False, None)
        ya = _pool(z, seg_x, l, pool_w, pool_scale, ya)
        yb = hyena(z, seg_x, l, dft_x, zfeat_x, yb)
        yc, _, _ = _retention(z, seg_x, l, dec_all, rope, s_f, s_b, True, yc)
        merged = _merge(ya, yb, yc, z, pa_b, pb_b, pc_b, rows)
        xs, xm = _res_ln(merged, wo_b, b_o, xs, mod, l, 2, ln1_g, ln1_b, (l, 3, 4), rows, seq, bsz,
                         "out_proj_ln1")
        hid, w2_b = _wsmm(xm, w_mlp1, b_mlp1, l, rows, True, "mlp_up", sides=(w_mlp2,))
        xs, xm = _res_ln(hid, w2_b, b_mlp2, xs, mod, l, 5, ln2_g, ln2_b,
                         None if last else (l + 1, 0, 1), rows, seq, bsz, "mlp_down_ln2")
    return xs.reshape(bsz, seq, D_MODEL)
```

```python
import functools
import math

import jax
import jax.numpy as jnp
from jax import lax
from jax.experimental import pallas as pl
from jax.experimental.pallas import tpu as pltpu

F32 = jnp.float32
BF16 = jnp.bfloat16

D_MODEL = 2048
DEPTH = 4
GRID_W = 64
D_POOL = D_MODEL // 4
POOL_WINDOWS = (2, 4, 8, 16)
POOL_GROUP = D_POOL // len(POOL_WINDOWS)
D_HYENA = D_MODEL // 4
FILTER_EMB = 33
FILTER_BANDS = (FILTER_EMB - 1) // 2
FILTER_ORDER = 64
FILTER_DECAY_TARGET = 1e-2
FILTER_FAST_PCT = 0.3
FILTER_SLOW_PCT = 1.5
RET_HEAD_DIM = 256
D_RET = D_MODEL // 2
RET_HEADS = D_RET // RET_HEAD_DIM
RET_CHUNK = 128
ROPE_BASE = 10000.0
ROPE_PAIRS = RET_HEAD_DIM // 4
N_BRANCH = 3
D_FF = 4 * D_MODEL
LN_EPS = 1e-5
GN_EPS = 1e-6
DEEPNORM_ALPHA = (2 * DEPTH) ** 0.25
O_POOL = 0
O_HY = O_POOL + D_POOL
O_Q = O_HY + 3 * D_HYENA
O_K = O_Q + D_RET
O_V = O_K + D_RET
O_G = O_V + D_RET
O_GATE = O_G + D_RET
D_IN = O_GATE + N_BRANCH * D_MODEL

LANE = 128
FEAT_PAD = LANE
POOL_PAD = 16
ADA_ROWS = 8
MIB = 1024 * 1024

TM_WS = (1536, 1024)
TM_MERGE = 1024
TN_WS = 1024
WS_SLABS = 4
TM_ROW = 512
TK_ROW = 2048
RES_SUB = 4
MERGE_SUB = 2
LN_ROWS = 16
HY_CT = 256
HY_FT = 512
HY_TM = 1024
DFT_RADIX = 64
RET_HPS = 2


def _params(semantics, vmem_mib):
    return pltpu.CompilerParams(dimension_semantics=semantics, vmem_limit_bytes=vmem_mib * MIB)


def _silu(v):
    return v * jax.nn.sigmoid(v)


def _layer_norm(r, g, b):
    mu = jnp.mean(r, axis=-1, keepdims=True)
    d = r - mu
    var = jnp.mean(d * d, axis=-1, keepdims=True)
    return d * lax.rsqrt(var + LN_EPS) * g + b


def _skip_ref(kern, idx):
    def wrapped(*refs):
        return kern(*refs[:idx], *refs[idx + 1:])
    return wrapped


def _seg_call(kern, *, prev, n_in, **kw):
    in_specs = list(kw.pop("in_specs")) + [pl.BlockSpec(memory_space=pl.ANY)]
    args = list(kw.pop("args")) + [prev]
    return pl.pallas_call(_skip_ref(kern, n_in), in_specs=in_specs,
                          input_output_aliases={n_in: 0}, **kw)(*args)


def _mod_index(layer, per_batch_tiles, bsz):
    return lambda i: layer * ADA_ROWS + jnp.minimum(i // per_batch_tiles, bsz)


def _ada_kernel(c_ref, w_ref, b_ref, o_ref):
    s = _silu(c_ref[...]).astype(BF16)
    o_ref[0] = jnp.dot(s, w_ref[0].astype(BF16), preferred_element_type=F32) + b_ref[0]


def _ada(cvec, w_ada, b_ada):
    tn = 1024
    n = w_ada.shape[2]
    return pl.pallas_call(
        _ada_kernel,
        grid=(DEPTH, n // tn),
        in_specs=[
            pl.BlockSpec((ADA_ROWS, D_MODEL), lambda l, j: (0, 0)),
            pl.BlockSpec((1, D_MODEL, tn), lambda l, j: (l, 0, j)),
            pl.BlockSpec((1, 1, tn), lambda l, j: (l, 0, j)),
        ],
        out_specs=pl.BlockSpec((1, ADA_ROWS, tn), lambda l, j: (l, 0, j)),
        out_shape=jax.ShapeDtypeStruct((DEPTH, ADA_ROWS, n), F32),
        compiler_params=_params(("parallel", "parallel"), 40),
        name="ada",
    )(cvec, w_ada, b_ada.reshape(DEPTH, 1, n))


def _mod0_kernel(x_ref, c_ref, sh_ref, sc_ref, xo_ref, xm_ref, *, n_lat_tiles):
    def emit(v):
        xo_ref[...] = v
        xm_ref[...] = (v * (1.0 + sc_ref[0]) + sh_ref[0]).astype(BF16)

    i = pl.program_id(0)

    @pl.when(i < n_lat_tiles)
    def _():
        emit(x_ref[...])

    @pl.when(i >= n_lat_tiles)
    def _():
        emit(c_ref[...])


def _mod0(x2d, c2d, mod, seq, bsz):
    n_lat, n_ctx = x2d.shape[0], c2d.shape[0]
    tm = TM_ROW
    nl = n_lat // tm
    midx = _mod_index(0, seq // tm, bsz)
    return pl.pallas_call(
        functools.partial(_mod0_kernel, n_lat_tiles=nl),
        grid=((n_lat + n_ctx) // tm,),
        in_specs=[pl.BlockSpec((tm, D_MODEL), lambda i: (jnp.minimum(i, nl - 1), 0)),
                  pl.BlockSpec((tm, D_MODEL), lambda i: (jnp.maximum(i - nl, 0), 0)),
                  pl.BlockSpec((1, 1, D_MODEL), lambda i: (midx(i), 0, 0)),
                  pl.BlockSpec((1, 1, D_MODEL), lambda i: (midx(i), 0, 1))],
        out_specs=[pl.BlockSpec((tm, D_MODEL), lambda i: (i, 0)),
                   pl.BlockSpec((tm, D_MODEL), lambda i: (i, 0))],
        out_shape=[jax.ShapeDtypeStruct((n_lat + n_ctx, D_MODEL), F32),
                   jax.ShapeDtypeStruct((n_lat + n_ctx, D_MODEL), BF16)],
        compiler_params=_params(("parallel",), 40),
        name="assemble_modulate",
    )(x2d, c2d, mod, mod)


def _regroup(a, b):
    lo = lax.broadcasted_iota(jnp.int32, a.shape, 1) < LANE // 2
    return (jnp.where(lo, a, pltpu.roll(b, LANE // 2, 1)),
            jnp.where(lo, pltpu.roll(a, LANE // 2, 1), b))


def _wsmm_kernel(*refs, sq_relu, side_outer, regroup_tiles):
    n_side = len(side_outer)
    x_ref, w_ref, b_ref = refs[:3]
    s_refs = refs[3:3 + n_side]
    o_ref = refs[3 + n_side]
    so_refs = refs[4 + n_side:4 + 2 * n_side]
    wb_ref, bb_ref = refs[4 + 2 * n_side:]
    j = pl.program_id(0)
    i = pl.program_id(1)
    first = i == 0
    tn = wb_ref.shape[1]
    for s_ref, so_ref, n_outer in zip(s_refs, so_refs, side_outer):
        @pl.when(jnp.logical_and(j < n_outer, i < WS_SLABS))
        def _(s_ref=s_ref, so_ref=so_ref):
            so_ref[...] = s_ref[0].astype(BF16)

    def plain():
        wb_ref[...] = w_ref[0].astype(BF16)
        bb_ref[...] = jnp.broadcast_to(b_ref[0], bb_ref.shape)

    def regrouped():
        b8 = jnp.broadcast_to(b_ref[0], bb_ref.shape)
        for c0 in range(0, tn, 2 * LANE):
            lo, hi = slice(c0, c0 + LANE), slice(c0 + LANE, c0 + 2 * LANE)
            wa, wc = _regroup(w_ref[0, :, lo], w_ref[0, :, hi])
            wb_ref[:, lo] = wa.astype(BF16)
            wb_ref[:, hi] = wc.astype(BF16)
            ba, bc = _regroup(b8[:, lo], b8[:, hi])
            bb_ref[:, lo] = ba
            bb_ref[:, hi] = bc

    if regroup_tiles:
        hit = functools.reduce(jnp.logical_or, [j == t for t in regroup_tiles])
        pl.when(jnp.logical_and(first, hit))(regrouped)
        pl.when(jnp.logical_and(first, jnp.logical_not(hit)))(plain)
    else:
        pl.when(first)(plain)

    y = jnp.dot(x_ref[...], wb_ref[...], preferred_element_type=F32) + bb_ref[0:1, :]
    if sq_relu:
        y = jnp.square(jnp.maximum(y, 0.0))
    o_ref[...] = y.astype(BF16)


def _wsmm(xm, w_all, b_all, layer, m_rows, sq_relu, name, sides=(), regroup_tiles=()):
    kdim, n = w_all.shape[1], w_all.shape[2]
    tm = next(t for t in TM_WS if m_rows % t == 0)
    tn = TN_WS
    n_j, n_i = n // tn, m_rows // tm
    assert n_i >= WS_SLABS
    side_outer = []
    in_specs = [pl.BlockSpec((tm, kdim), lambda j, i: (i, 0)),
                pl.BlockSpec((1, kdim, tn), lambda j, i: (layer, 0, j)),
                pl.BlockSpec((1, 1, tn), lambda j, i: (layer, 0, j))]
    args = [xm, w_all, b_all.reshape(DEPTH, 1, n)]
    out_specs = [pl.BlockSpec((tm, tn), lambda j, i: (i, j))]
    out_shape = [jax.ShapeDtypeStruct((xm.shape[0], n), BF16)]
    for side in sides:
        r, c = side.shape[1:]
        rows = 16
        while r % (rows * WS_SLABS) or r // (rows * WS_SLABS) > n_j:
            rows *= 2
            assert rows * WS_SLABS <= r
        n_outer = r // (rows * WS_SLABS)
        side_outer.append(n_outer)
        sidx = functools.partial(
            lambda j, i, n_outer: jnp.where(j < n_outer, j * WS_SLABS + jnp.minimum(i, WS_SLABS - 1),
                                            n_outer * WS_SLABS - 1),
            n_outer=n_outer)
        in_specs.append(pl.BlockSpec((1, rows, c), functools.partial(
            lambda j, i, sidx: (layer, sidx(j, i), 0), sidx=sidx)))
        args.append(side)
        out_specs.append(pl.BlockSpec((rows, c), functools.partial(
            lambda j, i, sidx: (sidx(j, i), 0), sidx=sidx)))
        out_shape.append(jax.ShapeDtypeStruct((r, c), BF16))
    return pl.pallas_call(
        functools.partial(_wsmm_kernel, sq_relu=sq_relu, side_outer=tuple(side_outer),
                          regroup_tiles=regroup_tiles),
        grid=(n_j, n_i),
        in_specs=in_specs,
        out_specs=out_specs,
        out_shape=out_shape,
        scratch_shapes=[pltpu.VMEM((kdim, tn), BF16), pltpu.VMEM((8, tn), F32)],
        compiler_params=_params(("parallel", "arbitrary"), 54),
        name=name,
    )(*args)


def _pool_kernel(z_ref, w_ref, s_ref, o_ref, pad_ref):
    seq = z_ref.shape[0]
    rows = seq + 2 * POOL_PAD
    zeros = jnp.zeros((POOL_PAD, POOL_GROUP), F32)
    pad_ref[0:POOL_PAD, :] = zeros
    pad_ref[POOL_PAD + seq:rows, :] = zeros
    t = lax.broadcasted_iota(jnp.int32, (seq, POOL_GROUP), 0)
    for g, win in enumerate(POOL_WINDOWS):
        cols = slice(g * POOL_GROUP, (g + 1) * POOL_GROUP)
        u = z_ref[:, cols].astype(F32)
        pad_ref[POOL_PAD:POOL_PAD + seq, :] = u
        w = pad_ref[...]
        w = pltpu.roll(w, 1, 0) + w
        width = 2
        while width < win:
            half = width // 2
            w = pltpu.roll(w, half, 0) + pltpu.roll(w, rows - half, 0)
            width *= 2
        half = win // 2
        count = jnp.minimum(t + half, seq) - jnp.maximum(t - half, 0)
        pooled = w[POOL_PAD:POOL_PAD + seq, :] / count.astype(F32) - u
        y = jnp.dot(pooled.astype(BF16), w_ref[0, g].astype(BF16), preferred_element_type=F32)
        o_ref[:, cols] = (y * s_ref[0, :, cols]).astype(BF16)


def _pool(z, seg, layer, pool_w, pool_scale, prev):
    bsz, seq, rb0 = seg
    ng = len(POOL_WINDOWS)
    return _seg_call(
        _pool_kernel, prev=prev, n_in=3,
        grid=(bsz,),
        in_specs=[pl.BlockSpec((seq, D_POOL), lambda bi: (rb0 + bi, O_POOL // D_POOL)),
                  pl.BlockSpec((1, ng, POOL_GROUP, POOL_GROUP), lambda bi: (layer, 0, 0, 0)),
                  pl.BlockSpec((1, 1, D_POOL), lambda bi: (layer, 0, 0))],
        out_specs=pl.BlockSpec((seq, D_POOL), lambda bi: (rb0 + bi, 0)),
        out_shape=jax.ShapeDtypeStruct((z.shape[0], D_POOL), BF16),
        scratch_shapes=[pltpu.VMEM((seq + 2 * POOL_PAD, POOL_GROUP), F32)],
        compiler_params=_params(("parallel",), 48),
        name="pool",
        args=[z, pool_w, pool_scale.reshape(DEPTH, 1, D_POOL)],
    )


def _conv3(u, w, b):
    seq = u.shape[0]
    t = lax.broadcasted_iota(jnp.int32, u.shape, 0)
    prev = jnp.where(t == 0, 0.0, pltpu.roll(u, 1, 0))
    nxt = jnp.where(t == seq - 1, 0.0, pltpu.roll(u, seq - 1, 0))
    return prev * w[0:1, :] + u * w[1:2, :] + nxt * w[2:3, :] + b


def _hy_pre_kernel(zv_ref, z0_ref, z1_ref, wv_ref, w0_ref, w1_ref, bv_ref, b0_ref, b1_ref,
                   uu_ref, x0_ref):
    v = _conv3(zv_ref[...].astype(F32), wv_ref[0], bv_ref[0])
    x1 = _conv3(z1_ref[...].astype(F32), w1_ref[0], b1_ref[0])
    uu_ref[0] = (v * x1).astype(BF16)
    x0_ref[0] = _conv3(z0_ref[...].astype(F32), w0_ref[0], b0_ref[0]).astype(BF16)


def _hy_pre(z, seg, layer, conv_w, conv_b):
    bsz, seq, rb0 = seg
    ct = HY_CT
    nct = D_HYENA // ct
    zoff = O_HY // ct
    zspec = [pl.BlockSpec((seq, ct), functools.partial(lambda bi, j, s: (rb0 + bi, zoff + s * nct + j), s=s))
             for s in range(3)]
    wspec = [pl.BlockSpec((1, 3, ct), functools.partial(lambda bi, j, s: (layer, 0, s * nct + j), s=s))
             for s in range(3)]
    bspec = [pl.BlockSpec((1, 1, ct), functools.partial(lambda bi, j, s: (layer, 0, s * nct + j), s=s))
             for s in range(3)]
    cb = conv_b.reshape(DEPTH, 1, 3 * D_HYENA)
    out = pl.BlockSpec((1, seq, ct), lambda bi, j: (bi, 0, j))
    return pl.pallas_call(
        _hy_pre_kernel,
        grid=(bsz, nct),
        in_specs=zspec + wspec + bspec,
        out_specs=[out, out],
        out_shape=[jax.ShapeDtypeStruct((bsz, seq, D_HYENA), BF16)] * 2,
        compiler_params=_params(("parallel", "parallel"), 48),
        name="hy_pre",
    )(z, z, z, conv_w, conv_w, conv_w, cb, cb, cb)


def _filt_kernel(zf_ref, w1_ref, b1_ref, f1_ref, w2_ref, b2_ref, f2_ref, w3_ref, b3_ref, f3_ref,
                 w4_ref, dl_ref, o_ref):
    tl = zf_ref.shape[0]
    zf = zf_ref[...]

    def dense(a, w_ref):
        return jnp.dot(a.astype(BF16), w_ref[0].astype(BF16), preferred_element_type=F32)

    hdn = jnp.sin(f1_ref[0] * (dense(zf, w1_ref) + b1_ref[0]))
    hdn = jnp.sin(f2_ref[0] * (dense(hdn, w2_ref) + b2_ref[0]))
    hdn = jnp.sin(f3_ref[0] * (dense(hdn, w3_ref) + b3_ref[0]))
    h = dense(hdn, w4_ref)
    decay = jnp.exp(-zf[:, 0:1] * jnp.abs(dl_ref[...]))
    row = lax.broadcasted_iota(jnp.int32, (tl, D_HYENA), 0) + pl.program_id(0) * tl
    o_ref[:, 0:D_HYENA] = (h[:, 0:D_HYENA] * decay).astype(BF16)
    o_ref[:, D_HYENA:] = jnp.where(row == 0, 0.0, h[:, D_HYENA:] * decay).astype(BF16)


def _filter_features(seq):
    t = jnp.linspace(0.0, 1.0, seq, dtype=F32)[:, None]
    w = 2.0 * math.pi * jnp.arange(seq, dtype=F32)[:, None] / seq
    f = jnp.linspace(1e-4, FILTER_BANDS - 1, FILTER_BANDS, dtype=F32)[None, :]
    z = jnp.concatenate([t, jnp.cos(f * w), -jnp.sin(f * w)], axis=-1)
    return jnp.pad(z, ((0, 0), (0, FEAT_PAD - FILTER_EMB)))


def _filter_deltas():
    max_decay = math.log(FILTER_DECAY_TARGET) / FILTER_FAST_PCT
    min_decay = math.log(FILTER_DECAY_TARGET) / FILTER_SLOW_PCT
    return jnp.linspace(min_decay, max_decay, D_HYENA, dtype=F32)[None, :]


def _filters(zfeat, deltas, layer, fp):
    seq = zfeat.shape[0]
    tl = min(256, seq)
    lsel = lambda shape: pl.BlockSpec((1,) + shape, lambda i: (layer,) + (0,) * len(shape))
    vec = lsel((1, FILTER_ORDER))
    sq = lsel((FILTER_ORDER, FILTER_ORDER))
    return pl.pallas_call(
        _filt_kernel,
        grid=(seq // tl,),
        in_specs=[pl.BlockSpec((tl, FEAT_PAD), lambda i: (i, 0)),
                  lsel((FEAT_PAD, FILTER_ORDER)), vec, vec, sq, vec, vec, sq, vec, vec,
                  lsel((FILTER_ORDER, 2 * D_HYENA)),
                  pl.BlockSpec((1, D_HYENA), lambda i: (0, 0))],
        out_specs=pl.BlockSpec((tl, 2 * D_HYENA), lambda i: (i, 0)),
        out_shape=jax.ShapeDtypeStruct((seq, 2 * D_HYENA), BF16),
        compiler_params=_params(("parallel",), 32),
        name="hy_filter",
    )(zfeat, fp['w1'], fp['b1'], fp['f1'], fp['w2'], fp['b2'], fp['f2'], fp['w3'], fp['b3'], fp['f3'],
      fp['w4'], deltas)


def _dft_kernel(ca_ref, sa_ref, cb_ref, sb_ref, fwd_ref, inv_ref):
    ca, sa = ca_ref[0], sa_ref[0]
    cb, sb = cb_ref[...], sb_ref[...]
    cosb = ca * cb - sa * sb
    sinb = sa * cb + ca * sb
    row = lax.broadcasted_iota(jnp.int32, cosb.shape, 0) + pl.program_id(0) * cosb.shape[0]
    col = lax.broadcasted_iota(jnp.int32, cosb.shape, 1)
    alt = lambda idx: jnp.where(jnp.bitwise_and(idx, 1) == 0, 1.0, -1.0)
    fwd_ref[0] = cosb.astype(BF16)
    fwd_ref[1] = jnp.where(row == 0, alt(col), sinb).astype(BF16)
    inv_ref[...] = jnp.where(col == 0, alt(row), sinb).astype(BF16)


def _dft_matrices(seq):
    n = 2 * seq
    radix = DFT_RADIX
    t = jnp.arange(seq, dtype=jnp.int32)[None, :]
    ang = lambda f: ((f * t) % n).astype(F32) * (2.0 * math.pi / n)
    ang_a = ang(radix * jnp.arange(seq // radix, dtype=jnp.int32)[:, None])[:, None, :]
    ang_b = ang(jnp.arange(radix, dtype=jnp.int32)[:, None])
    coarse = pl.BlockSpec((1, 1, seq), lambda i: (i, 0, 0))
    fine = pl.BlockSpec((radix, seq), lambda i: (0, 0))
    return pl.pallas_call(
        _dft_kernel,
        grid=(seq // radix,),
        in_specs=[coarse, coarse, fine, fine],
        out_specs=[pl.BlockSpec((2, radix, seq), lambda i: (0, i, 0)),
                   pl.BlockSpec((radix, seq), lambda i: (i, 0))],
        out_shape=[jax.ShapeDtypeStruct((2, seq, seq), BF16), jax.ShapeDtypeStruct((seq, seq), BF16)],
        compiler_params=_params(("parallel",), 32),
        name="dft_tables",
    )(jnp.cos(ang_a), jnp.sin(ang_a), jnp.cos(ang_b), jnp.sin(ang_b))


def _mm_kernel(a_ref, b_ref, o_ref):
    o_ref[0] = jnp.dot(a_ref[0], b_ref[...], preferred_element_type=F32)


def _filter_spectrum(fwd, hcat):
    _, seq, _ = fwd.shape
    n = hcat.shape[1]
    tm = min(512, seq)
    tn = 512
    return pl.pallas_call(
        _mm_kernel,
        grid=(2, seq // tm, n // tn),
        in_specs=[pl.BlockSpec((1, tm, seq), lambda h, i, j: (h, i, 0)),
                  pl.BlockSpec((seq, tn), lambda h, i, j: (0, j))],
        out_specs=pl.BlockSpec((1, tm, tn), lambda h, i, j: (h, i, j)),
        out_shape=jax.ShapeDtypeStruct((2, seq, n), F32),
        compiler_params=_params(("parallel", "parallel", "parallel"), 32),
        name="hy_filter_dft",
    )(fwd, hcat)


def _hy_fwd_kernel(f_ref, uu_ref, kf_ref, kb_ref, d_ref, y_ref, *, n_fft):
    tm = f_ref.shape[1]
    uu = uu_ref[0]
    a = jnp.dot(f_ref[0], uu, preferred_element_type=F32)
    b = jnp.dot(f_ref[1], uu, preferred_element_type=F32)
    row0 = (lax.broadcasted_iota(jnp.int32, a.shape, 0) + pl.program_id(0) * tm) == 0
    ka = kf_ref[0] + kb_ref[0] + d_ref[0]
    kb_sum = kf_ref[1] + kb_ref[1] + d_ref[0]
    kb_dif = kf_ref[1] - kb_ref[1]
    ya = jnp.where(row0, a * ka, a * ka - b * kb_dif)
    yb = jnp.where(row0, b * kb_sum, a * kb_dif + b * ka)
    wgt = jnp.where(row0, 1.0 / n_fft, 2.0 / n_fft)
    y_ref[0, 0] = (ya * wgt).astype(BF16)
    y_ref[0, 1] = (yb * wgt).astype(BF16)


def _hy_fwd(fwd, uu, kspec, layer, hyena_d):
    bsz, seq, _ = uu.shape
    ct = HY_FT
    nct = D_HYENA // ct
    tm = min(HY_TM, seq)
    return pl.pallas_call(
        functools.partial(_hy_fwd_kernel, n_fft=2 * seq),
        grid=(seq // tm, nct, bsz),
        in_specs=[pl.BlockSpec((2, tm, seq), lambda i, j, bi: (0, i, 0)),
                  pl.BlockSpec((1, seq, ct), lambda i, j, bi: (bi, 0, j)),
                  pl.BlockSpec((2, tm, ct), lambda i, j, bi: (0, i, j)),
                  pl.BlockSpec((2, tm, ct), lambda i, j, bi: (0, i, nct + j)),
                  pl.BlockSpec((1, 1, ct), lambda i, j, bi: (layer, 0, j))],
        out_specs=pl.BlockSpec((1, 2, tm, ct), lambda i, j, bi: (bi, 0, i, j)),
        out_shape=jax.ShapeDtypeStruct((bsz, 2, seq, D_HYENA), BF16),
        compiler_params=_params(("parallel", "parallel", "parallel"), 56),
        name="hy_fwd_dft",
    )(fwd, uu, kspec, kspec, hyena_d.reshape(DEPTH, 1, D_HYENA))


def _hy_inv_kernel(fc_ref, fs_ref, y_ref, x0_ref, o_ref):
    y = (jnp.dot(fc_ref[0], y_ref[0, 0], preferred_element_type=F32)
         + jnp.dot(fs_ref[...], y_ref[0, 1], preferred_element_type=F32))
    o_ref[...] = (y * x0_ref[0].astype(F32)).astype(BF16)


def _hy_inv(fwd, inv, yspec, x0, seg, m_total, prev):
    bsz, seq, rb0 = seg
    ct = HY_FT
    nct = D_HYENA // ct
    tm = min(HY_TM, seq)
    per = seq // tm
    return _seg_call(
        _hy_inv_kernel, prev=prev, n_in=4,
        grid=(per, nct, bsz),
        in_specs=[pl.BlockSpec((1, tm, seq), lambda i, j, bi: (0, i, 0)),
                  pl.BlockSpec((tm, seq), lambda i, j, bi: (i, 0)),
                  pl.BlockSpec((1, 2, seq, ct), lambda i, j, bi: (bi, 0, 0, j)),
                  pl.BlockSpec((1, tm, ct), lambda i, j, bi: (bi, i, j))],
        out_specs=pl.BlockSpec((tm, ct), lambda i, j, bi: ((rb0 + bi) * per + i, j)),
        out_shape=jax.ShapeDtypeStruct((m_total, D_HYENA), BF16),
        compiler_params=_params(("parallel", "parallel", "parallel"), 56),
        name="hy_inv_dft",
        args=[fwd, inv, yspec, x0],
    )


def _ret_kernel(*refs, use_rope, need_out):
    refs = list(refs)
    dec_ref, q_ref, k_ref, v_ref, g_ref = refs[:5]
    refs = refs[5:]
    if use_rope:
        cos_ref, sin_ref = refs[:2]
        refs = refs[2:]
    sf0_ref, sb0_ref = refs[:2]
    refs = refs[2:]
    if need_out:
        y_ref = refs[0]
        refs = refs[1:]
    sfo_ref, sbo_ref, qs, kst, accf, accb, st = refs

    seq = k_ref.shape[0]
    csz = RET_CHUNK
    n_chunks = seq // csz
    half = n_chunks // 2
    hd = RET_HEAD_DIM
    heads = range(RET_HPS)

    pos = lax.broadcasted_iota(jnp.int32, (csz, hd), 0).astype(F32)
    ii = lax.broadcasted_iota(jnp.int32, (csz, csz), 0)
    jj = lax.broadcasted_iota(jnp.int32, (csz, csz), 1)
    rel = (ii - jj).astype(F32)
    qdec_f, vdec_f, qdec_b, vdec_b, cdec_f, cdec_b, mask = [], [], [], [], [], [], []
    for hh in heads:
        lg = jnp.log1p(-jnp.exp(dec_ref[hh]))
        lgf = lg[0:1, :]
        lgb = lg[1:2, :]
        qdec_f.append(jnp.exp((pos + 1.0) * lgf))
        vdec_f.append(jnp.exp((csz - 1.0 - pos) * lgf))
        qdec_b.append(jnp.exp((csz - pos) * lgb))
        vdec_b.append(jnp.exp(pos * lgb))
        cdec_f.append(jnp.exp(csz * lgf))
        cdec_b.append(jnp.exp(csz * lgb))
        mask.append(jnp.where(rel >= 0, jnp.exp(jnp.maximum(rel, 0.0) * lgf[:, :csz]), 0.0)
                    + jnp.where(rel <= 0, jnp.exp(jnp.maximum(-rel, 0.0) * lgb[:, :csz]), 0.0))

    def chunk_rows(c):
        return pl.ds(pl.multiple_of(c * csz, csz), csz)

    def rope(x, rows):
        if not use_rope:
            return x
        a, b = x[:, :LANE], x[:, LANE:]
        cs, sn = cos_ref[rows, :], sin_ref[rows, :]
        return jnp.concatenate([a * cs - b * sn, b * cs + a * sn], axis=1)

    def prep(c, carry):
        rows = chunk_rows(c)
        for hh in heads:
            cols = slice(hh * hd, (hh + 1) * hd)
            kc = rope(k_ref[rows, cols].astype(F32) * (hd ** -0.5), rows)
            kst[c, hh] = kc.T.astype(BF16)
            if need_out:
                qs[rows, cols] = rope(q_ref[rows, cols].astype(F32), rows).astype(BF16)
        return carry

    lax.fori_loop(0, n_chunks, prep, 0, unroll=2)

    for hh in heads:
        st[hh, 0] = sf0_ref[0, hh]
        st[hh, 1] = sb0_ref[0, hh]

    def visit(hh, direction, c, finish):
        cols = slice(hh * hd, (hh + 1) * hd)
        rows = chunk_rows(c)
        kt = kst[c, hh]
        v = v_ref[rows, cols]
        state = st[hh, direction]
        if need_out:
            q = qs[rows, cols]
            carried = jnp.dot(q, state.astype(BF16), preferred_element_type=F32)
            if direction == 0:
                s = jnp.dot(q, kt, preferred_element_type=F32)
                o = jnp.dot((s * mask[hh]).astype(BF16), v, preferred_element_type=F32)
                o = o + qdec_f[hh] * carried
                mine, other = accf, accb
            else:
                o = qdec_b[hh] * carried
                mine, other = accb, accf
            if finish:
                o = o + other[rows, cols]
                mu = jnp.mean(o, axis=-1, keepdims=True)
                d = o - mu
                var = jnp.mean(d * d, axis=-1, keepdims=True)
                gate = _silu(g_ref[rows, cols].astype(F32))
                y_ref[rows, cols] = (gate * (d * lax.rsqrt(var + GN_EPS))).astype(BF16)
            else:
                mine[rows, cols] = o
        vdec, cdec = (vdec_f, cdec_f) if direction == 0 else (vdec_b, cdec_b)
        st[hh, direction] = state * cdec[hh] + jnp.dot(
            kt, (v.astype(F32) * vdec[hh]).astype(BF16), preferred_element_type=F32)

    def scan(finish):
        def body(i, carry):
            for hh in heads:
                visit(hh, 0, i, finish)
                visit(hh, 1, n_chunks - 1 - i, finish)
            return carry
        return body

    lax.fori_loop(0, half, scan(False), 0, unroll=min(2, half))
    lax.fori_loop(half, n_chunks, scan(need_out), 0, unroll=min(2, half))

    for hh in heads:
        sfo_ref[0, hh] = st[hh, 0]
        sbo_ref[0, hh] = st[hh, 1]


def _retention(z, seg, layer, dec_all, rope, s_f, s_b, need_out, prev):
    bsz, seq, rb0 = seg
    assert (seq // RET_CHUNK) % 2 == 0
    hd = RET_HEAD_DIM
    hps = RET_HPS
    wide = hps * hd
    use_rope = rope is not None
    col = lambda off: pl.BlockSpec((seq, wide), lambda bi, h: (rb0 + bi, off // wide + h))
    state = pl.BlockSpec((1, hps, hd, hd), lambda bi, h: (bi, h, 0, 0))
    in_specs = [pl.BlockSpec((None, hps, 2, hd), lambda bi, h: (layer, h, 0, 0)),
                col(O_Q), col(O_K), col(O_V), col(O_G)]
    args = [dec_all, z, z, z, z]
    if use_rope:
        in_specs += [pl.BlockSpec((seq, LANE), lambda bi, h: (0, 0))] * 2
        args += list(rope)
    in_specs += [state, state]
    args += [s_f, s_b]
    out_specs = [state, state]
    out_shape = [jax.ShapeDtypeStruct((bsz, RET_HEADS, hd, hd), F32)] * 2
    n_chunks = seq // RET_CHUNK
    scratch = [pltpu.VMEM((seq, wide), BF16), pltpu.VMEM((n_chunks, hps, hd, RET_CHUNK), BF16),
               pltpu.VMEM((seq, wide), F32), pltpu.VMEM((seq, wide), F32),
               pltpu.VMEM((hps, 2, hd, hd), F32)]
    kw = dict(grid=(bsz, RET_HEADS // hps), scratch_shapes=scratch,
              compiler_params=_params(("parallel", "parallel"), 56))
    kern = functools.partial(_ret_kernel, use_rope=use_rope, need_out=need_out)
    if not need_out:
        res = pl.pallas_call(kern, in_specs=in_specs, out_specs=out_specs, out_shape=out_shape,
                             name="retention_state", **kw)(*args)
        return None, res[0], res[1]
    out_specs = [pl.BlockSpec((seq, wide), lambda bi, h: (rb0 + bi, h))] + out_specs
    out_shape = [jax.ShapeDtypeStruct((z.shape[0], D_RET), BF16)] + out_shape
    res = _seg_call(kern, prev=prev, n_in=len(args), in_specs=in_specs, out_specs=out_specs,
                    out_shape=out_shape, name="retention", args=args, **kw)
    return res[0], res[1], res[2]


def _rope_tables(seq):
    rows = seq // GRID_W
    row = jnp.repeat(jnp.arange(rows, dtype=F32), GRID_W)
    colp = jnp.tile(jnp.arange(GRID_W, dtype=F32), rows)
    inv = ROPE_BASE ** (-jnp.arange(ROPE_PAIRS, dtype=F32) / ROPE_PAIRS)
    ang_r = row[:, None] * inv[None, :]
    ang_c = colp[:, None] * inv[None, :]
    return (jnp.concatenate([jnp.cos(ang_r), jnp.cos(ang_c)], axis=-1),
            jnp.concatenate([jnp.sin(ang_r), jnp.sin(ang_c)], axis=-1))


def _merge_kernel(ya_ref, yb_ref, yc_ref, ga_ref, gb_ref, gc_ref, pa_ref, pb_ref, pc_ref, o_ref):
    def branch(g_ref, y_ref, p_ref, rows):
        return (jax.nn.sigmoid(g_ref[rows, :].astype(F32))
                * jnp.dot(y_ref[rows, :], p_ref[...], preferred_element_type=F32))

    sub = o_ref.shape[0] // MERGE_SUB
    for s in range(MERGE_SUB):
        rows = slice(s * sub, (s + 1) * sub)
        m = (branch(ga_ref, ya_ref, pa_ref, rows) + branch(gb_ref, yb_ref, pb_ref, rows)
             + branch(gc_ref, yc_ref, pc_ref, rows))
        o_ref[rows, :] = m.astype(BF16)


def _merge(ya, yb, yc, z, p_a, p_b, p_c, m_rows):
    tm, tn = TM_MERGE, TN_WS
    goff = O_GATE // tn
    gstep = D_MODEL // tn
    act = lambda width: pl.BlockSpec((tm, width), lambda i, j: (i, 0))
    gate = lambda br: pl.BlockSpec((tm, tn), lambda i, j: (i, goff + br * gstep + j))
    wgt = lambda rows: pl.BlockSpec((rows, tn), lambda i, j: (0, j))
    return pl.pallas_call(
        _merge_kernel,
        grid=(m_rows // tm, D_MODEL // tn),
        in_specs=[act(D_POOL), act(D_HYENA), act(D_RET), gate(0), gate(1), gate(2),
                  wgt(D_POOL), wgt(D_HYENA), wgt(D_RET)],
        out_specs=pl.BlockSpec((tm, tn), lambda i, j: (i, j)),
        out_shape=jax.ShapeDtypeStruct((m_rows, D_MODEL), BF16),
        compiler_params=_params(("parallel", "arbitrary"), 48),
        name="merge",
    )(ya, yb, yc, z, z, z, p_a, p_b, p_c)


def _res_ln_kernel(*refs, emit_xm, n_k):
    a_ref, w_ref, b_ref, x_ref, gt_ref, g_ref, be_ref = refs[:7]
    refs = refs[7:]
    if emit_xm:
        sh_ref, sc_ref, xo_ref, xm_ref, acc_ref = refs
    else:
        xo_ref, acc_ref = refs
    k = pl.program_id(1)
    sub = x_ref.shape[0] // RES_SUB

    def finish(r0):
        rr = slice(r0, r0 + LN_ROWS)
        r = DEEPNORM_ALPHA * x_ref[rr, :] + gt_ref[0] * (acc_ref[rr, :] + b_ref[0])
        xn = _layer_norm(r, g_ref[0], be_ref[0])
        xo_ref[rr, :] = xn
        if emit_xm:
            xm_ref[rr, :] = (xn * (1.0 + sc_ref[0]) + sh_ref[0]).astype(BF16)

    def step(first, last):
        for s in range(RES_SUB):
            rows = slice(s * sub, (s + 1) * sub)
            part = jnp.dot(a_ref[rows, :], w_ref[...], preferred_element_type=F32)
            if first:
                acc_ref[rows, :] = part
            else:
                acc_ref[rows, :] += part
            if last:
                for r0 in range(s * sub, (s + 1) * sub, LN_ROWS):
                    finish(r0)

    if n_k == 1:
        step(True, True)
    else:
        pl.when(k == 0)(lambda: step(True, False))
        pl.when(jnp.logical_and(k > 0, k < n_k - 1))(lambda: step(False, False))
        pl.when(k == n_k - 1)(lambda: step(False, True))


def _res_ln(a, w, b_all, x, mod, layer, gate_blk, ln_g, ln_b, next_mod, m_rows, seq, bsz, name):
    kdim = a.shape[1]
    tm = TM_ROW
    tk = min(TK_ROW, kdim)
    per = seq // tm
    gidx = _mod_index(layer, per, bsz)
    vec = lambda arr: arr.reshape(DEPTH, 1, D_MODEL)
    lvec = pl.BlockSpec((1, 1, D_MODEL), lambda i, k: (layer, 0, 0))
    in_specs = [pl.BlockSpec((tm, tk), lambda i, k: (i, k)),
                pl.BlockSpec((tk, D_MODEL), lambda i, k: (k, 0)),
                lvec,
                pl.BlockSpec((tm, D_MODEL), lambda i, k: (i, 0)),
                pl.BlockSpec((1, 1, D_MODEL), lambda i, k: (gidx(i), 0, gate_blk)),
                lvec, lvec]
    args = [a, w, vec(b_all), x, mod, vec(ln_g), vec(ln_b)]
    row_out = pl.BlockSpec((tm, D_MODEL), lambda i, k: (i, 0))
    out_specs = [row_out]
    out_shape = [jax.ShapeDtypeStruct((m_rows, D_MODEL), F32)]
    if next_mod is not None:
        nl, sh_blk, sc_blk = next_mod
        nidx = _mod_index(nl, per, bsz)
        in_specs += [pl.BlockSpec((1, 1, D_MODEL), lambda i, k: (nidx(i), 0, sh_blk)),
                     pl.BlockSpec((1, 1, D_MODEL), lambda i, k: (nidx(i), 0, sc_blk))]
        args += [mod, mod]
        out_specs.append(row_out)
        out_shape.append(jax.ShapeDtypeStruct((m_rows, D_MODEL), BF16))
    res = pl.pallas_call(
        functools.partial(_res_ln_kernel, emit_xm=next_mod is not None, n_k=kdim // tk),
        grid=(m_rows // tm, kdim // tk),
        in_specs=in_specs,
        out_specs=out_specs,
        out_shape=out_shape,
        scratch_shapes=[pltpu.VMEM((tm, D_MODEL), F32)],
        compiler_params=_params(("parallel", "arbitrary"), 56),
        name=name,
    )(*args)
    return (res[0], res[1]) if next_mod is not None else (res[0], None)


def kernel(x, c, ctx, c_ctx, w_ada, b_ada, w_in, b_in, conv_w, conv_b, pool_w, pool_scale, filt_w1, filt_b1, filt_f1, filt_w2, filt_b2, filt_f2, filt_w3, filt_b3, filt_f3, filt_w4, hyena_d, ret_decay, p_a, p_b, p_c, w_o, b_o, ln1_g, ln1_b, w_mlp1, b_mlp1, w_mlp2, b_mlp2, ln2_g, ln2_b):
    bsz, seq, _ = x.shape
    ctx_len = ctx.shape[1]
    assert x.shape == (bsz, seq, D_MODEL) and ctx.shape == (bsz, ctx_len, D_MODEL)
    assert seq % RET_CHUNK == 0 and ctx_len % RET_CHUNK == 0 and seq % GRID_W == 0
    assert bsz + 1 <= ADA_ROWS and seq % ctx_len == 0
    n_lat, n_ctx = bsz * seq, bsz * ctx_len
    m_total = n_lat + n_ctx
    assert seq % TM_MERGE == 0 and n_ctx % TM_MERGE == 0 and seq % TM_ROW == 0 and n_ctx % TM_ROW == 0
    assert O_Q % TN_WS == 0 and O_V % TN_WS == 0
    seg_x = (bsz, seq, 0)
    seg_c = (bsz, ctx_len, n_lat // ctx_len)

    cvec = jnp.concatenate([c, c_ctx[None, :], jnp.zeros((ADA_ROWS - bsz - 1, D_MODEL), F32)], axis=0)
    mod = _ada(cvec, w_ada, b_ada).reshape(DEPTH * ADA_ROWS, 1, 6 * D_MODEL)

    deltas = _filter_deltas()
    dft_x, dft_c = _dft_matrices(seq), _dft_matrices(ctx_len)
    zfeat_x, zfeat_c = _filter_features(seq), _filter_features(ctx_len)
    rope = _rope_tables(seq)
    zero_state = jnp.zeros((bsz, RET_HEADS, RET_HEAD_DIM, RET_HEAD_DIM), F32)

    row3 = lambda a: a.reshape(DEPTH, 1, -1)
    fp = {'w1': jnp.pad(filt_w1, ((0, 0), (0, FEAT_PAD - FILTER_EMB), (0, 0))),
          'b1': row3(filt_b1), 'f1': row3(filt_f1), 'w2': filt_w2, 'b2': row3(filt_b2), 'f2': row3(filt_f2),
          'w3': filt_w3, 'b3': row3(filt_b3), 'f3': row3(filt_f3), 'w4': filt_w4}
    dec_all = jnp.broadcast_to(jnp.swapaxes(ret_decay, 1, 2)[:, :, :, None],
                               (DEPTH, RET_HEADS, 2, RET_HEAD_DIM))
    xs, xm = _mod0(x.reshape(n_lat, D_MODEL), ctx.reshape(n_ctx, D_MODEL), mod, seq, bsz)

    def hyena(z, seg, l, dft, zfeat, prev):
        fwd, inv = dft
        uu, x0 = _hy_pre(z, seg, l, conv_w, conv_b)
        kspec = _filter_spectrum(fwd, _filters(zfeat, deltas, l, fp))
        return _hy_inv(fwd, inv, _hy_fwd(fwd, uu, kspec, l, hyena_d), x0, seg, m_total, prev)

    ya = jnp.zeros((m_total, D_POOL), BF16)
    yb = jnp.zeros((m_total, D_HYENA), BF16)
    yc = jnp.zeros((m_total, D_RET), BF16)
    qk_tiles = tuple(range(O_Q // TN_WS, O_V // TN_WS))
    for l in range(DEPTH):
        last = l == DEPTH - 1
        rows = n_lat if last else m_total
        z, wo_b, pa_b, pb_b, pc_b = _wsmm(xm, w_in, b_in, l, m_total, False, "in_proj",
                                          sides=(w_o, p_a, p_b, p_c), regroup_tiles=qk_tiles)
        if not last:
            ya = _pool(z, seg_c, l, pool_w, pool_scale, ya)
            yb = hyena(z, seg_c, l, dft_c, zfeat_c, yb)
            yc, s_f, s_b = _retention(z, seg_c, l, dec_all, None, zero_state, zero_state, True, yc)
        else:
            _, s_f, s_b = _retention(z, seg_c, l, dec_all, None, zero_state, zero_state, False, None)
        ya = _pool(z, seg_x, l, pool_w, pool_scale, ya)
        yb = hyena(z, seg_x, l, dft_x, zfeat_x, yb)
        yc, _, _ = _retention(z, seg_x, l, dec_all, rope, s_f, s_b, True, yc)
        merged = _merge(ya, yb, yc, z, pa_b, pb_b, pc_b, rows)
        xs, xm = _res_ln(merged, wo_b, b_o, xs, mod, l, 2, ln1_g, ln1_b, (l, 3, 4), rows, seq, bsz,
                         "out_proj_ln1")
        hid, w2_b = _wsmm(xm, w_mlp1, b_mlp1, l, rows, True, "mlp_up", sides=(w_mlp2,))
        xs, xm = _res_ln(hid, w2_b, b_mlp2, xs, mod, l, 5, ln2_g, ln2_b,
                         None if last else (l + 1, 0, 1), rows, seq, bsz, "mlp_down_ln2")
    return xs.reshape(bsz, seq, D_MODEL)
```

```python
import functools
import math

import jax
import jax.numpy as jnp
from jax import lax
from jax.experimental import pallas as pl
from jax.experimental.pallas import tpu as pltpu

F32 = jnp.float32
BF16 = jnp.bfloat16

D_MODEL = 2048
DEPTH = 4
GRID_W = 64
D_POOL = D_MODEL // 4
POOL_WINDOWS = (2, 4, 8, 16)
POOL_GROUP = D_POOL // len(POOL_WINDOWS)
D_HYENA = D_MODEL // 4
FILTER_EMB = 33
FILTER_BANDS = (FILTER_EMB - 1) // 2
FILTER_ORDER = 64
FILTER_DECAY_TARGET = 1e-2
FILTER_FAST_PCT = 0.3
FILTER_SLOW_PCT = 1.5
RET_HEAD_DIM = 256
D_RET = D_MODEL // 2
RET_HEADS = D_RET // RET_HEAD_DIM
RET_CHUNK = 128
ROPE_BASE = 10000.0
ROPE_PAIRS = RET_HEAD_DIM // 4
N_BRANCH = 3
D_FF = 4 * D_MODEL
LN_EPS = 1e-5
GN_EPS = 1e-6
DEEPNORM_ALPHA = (2 * DEPTH) ** 0.25
O_POOL = 0
O_HY = O_POOL + D_POOL
O_Q = O_HY + 3 * D_HYENA
O_K = O_Q + D_RET
O_V = O_K + D_RET
O_G = O_V + D_RET
O_GATE = O_G + D_RET
D_IN = O_GATE + N_BRANCH * D_MODEL

LANE = 128
FEAT_PAD = LANE
POOL_PAD = 16
ADA_ROWS = 8
MIB = 1024 * 1024

TM_WS = (1536, 1024)
TM_MERGE = 1024
TN_WS = 1024
WS_SLABS = 4
TM_ROW = 512
TK_ROW = 2048
RES_SUB = 4
MERGE_SUB = 2
LN_ROWS = 16
HY_CT = 256
HY_FT = 512
HY_TM = 1024
HY_SUB = 2
DFT_RADIX = 64
RET_HPS = 2
RET_BLOCK = 256


def _params(semantics, vmem_mib):
    return pltpu.CompilerParams(dimension_semantics=semantics, vmem_limit_bytes=vmem_mib * MIB)


def _silu(v):
    return v * jax.nn.sigmoid(v)


def _layer_norm(r, g, b):
    mu = jnp.mean(r, axis=-1, keepdims=True)
    d = r - mu
    var = jnp.mean(d * d, axis=-1, keepdims=True)
    return d * lax.rsqrt(var + LN_EPS) * g + b


def _skip_ref(kern, idx):
    def wrapped(*refs):
        return kern(*refs[:idx], *refs[idx + 1:])
    return wrapped


def _seg_call(kern, *, prev, n_in, **kw):
    in_specs = list(kw.pop("in_specs")) + [pl.BlockSpec(memory_space=pl.ANY)]
    args = list(kw.pop("args")) + [prev]
    return pl.pallas_call(_skip_ref(kern, n_in), in_specs=in_specs,
                          input_output_aliases={n_in: 0}, **kw)(*args)


def _mod_index(layer, per_batch_tiles, bsz):
    return lambda i: layer * ADA_ROWS + jnp.minimum(i // per_batch_tiles, bsz)


def _ada_kernel(c_ref, w_ref, b_ref, o_ref):
    s = _silu(c_ref[...]).astype(BF16)
    o_ref[0] = jnp.dot(s, w_ref[0].astype(BF16), preferred_element_type=F32) + b_ref[0]


def _ada(cvec, w_ada, b_ada):
    tn = 1024
    n = w_ada.shape[2]
    return pl.pallas_call(
        _ada_kernel,
        grid=(DEPTH, n // tn),
        in_specs=[
            pl.BlockSpec((ADA_ROWS, D_MODEL), lambda l, j: (0, 0)),
            pl.BlockSpec((1, D_MODEL, tn), lambda l, j: (l, 0, j)),
            pl.BlockSpec((1, 1, tn), lambda l, j: (l, 0, j)),
        ],
        out_specs=pl.BlockSpec((1, ADA_ROWS, tn), lambda l, j: (l, 0, j)),
        out_shape=jax.ShapeDtypeStruct((DEPTH, ADA_ROWS, n), F32),
        compiler_params=_params(("parallel", "parallel"), 40),
        name="ada",
    )(cvec, w_ada, b_ada.reshape(DEPTH, 1, n))


def _mod0_kernel(x_ref, c_ref, sh_ref, sc_ref, xo_ref, xm_ref, *, n_lat_tiles):
    def emit(v):
        xo_ref[...] = v
        xm_ref[...] = (v * (1.0 + sc_ref[0]) + sh_ref[0]).astype(BF16)

    i = pl.program_id(0)

    @pl.when(i < n_lat_tiles)
    def _():
        emit(x_ref[...])

    @pl.when(i >= n_lat_tiles)
    def _():
        emit(c_ref[...])


def _mod0(x2d, c2d, mod, seq, bsz):
    n_lat, n_ctx = x2d.shape[0], c2d.shape[0]
    tm = TM_ROW
    nl = n_lat // tm
    midx = _mod_index(0, seq // tm, bsz)
    return pl.pallas_call(
        functools.partial(_mod0_kernel, n_lat_tiles=nl),
        grid=((n_lat + n_ctx) // tm,),
        in_specs=[pl.BlockSpec((tm, D_MODEL), lambda i: (jnp.minimum(i, nl - 1), 0)),
                  pl.BlockSpec((tm, D_MODEL), lambda i: (jnp.maximum(i - nl, 0), 0)),
                  pl.BlockSpec((1, 1, D_MODEL), lambda i: (midx(i), 0, 0)),
                  pl.BlockSpec((1, 1, D_MODEL), lambda i: (midx(i), 0, 1))],
        out_specs=[pl.BlockSpec((tm, D_MODEL), lambda i: (i, 0)),
                   pl.BlockSpec((tm, D_MODEL), lambda i: (i, 0))],
        out_shape=[jax.ShapeDtypeStruct((n_lat + n_ctx, D_MODEL), F32),
                   jax.ShapeDtypeStruct((n_lat + n_ctx, D_MODEL), BF16)],
        compiler_params=_params(("parallel",), 40),
        name="assemble_modulate",
    )(x2d, c2d, mod, mod)


def _regroup(a, b):
    lo = lax.broadcasted_iota(jnp.int32, a.shape, 1) < LANE // 2
    return (jnp.where(lo, a, pltpu.roll(b, LANE // 2, 1)),
            jnp.where(lo, pltpu.roll(a, LANE // 2, 1), b))


def _wsmm_kernel(*refs, sq_relu, side_outer, regroup_tiles):
    n_side = len(side_outer)
    x_ref, w_ref, b_ref = refs[:3]
    s_refs = refs[3:3 + n_side]
    o_ref = refs[3 + n_side]
    so_refs = refs[4 + n_side:4 + 2 * n_side]
    wb_ref, bb_ref = refs[4 + 2 * n_side:]
    j = pl.program_id(0)
    i = pl.program_id(1)
    first = i == 0
    tn = wb_ref.shape[1]
    for s_ref, so_ref, n_outer in zip(s_refs, so_refs, side_outer):
        @pl.when(jnp.logical_and(j < n_outer, i < WS_SLABS))
        def _(s_ref=s_ref, so_ref=so_ref):
            so_ref[...] = s_ref[0].astype(BF16)

    def plain():
        wb_ref[...] = w_ref[0].astype(BF16)
        bb_ref[...] = jnp.broadcast_to(b_ref[0], bb_ref.shape)

    def regrouped():
        b8 = jnp.broadcast_to(b_ref[0], bb_ref.shape)
        for c0 in range(0, tn, 2 * LANE):
            lo, hi = slice(c0, c0 + LANE), slice(c0 + LANE, c0 + 2 * LANE)
            wa, wc = _regroup(w_ref[0, :, lo], w_ref[0, :, hi])
            wb_ref[:, lo] = wa.astype(BF16)
            wb_ref[:, hi] = wc.astype(BF16)
            ba, bc = _regroup(b8[:, lo], b8[:, hi])
            bb_ref[:, lo] = ba
            bb_ref[:, hi] = bc

    if regroup_tiles:
        hit = functools.reduce(jnp.logical_or, [j == t for t in regroup_tiles])
        pl.when(jnp.logical_and(first, hit))(regrouped)
        pl.when(jnp.logical_and(first, jnp.logical_not(hit)))(plain)
    else:
        pl.when(first)(plain)

    y = jnp.dot(x_ref[...], wb_ref[...], preferred_element_type=F32) + bb_ref[0:1, :]
    if sq_relu:
        y = jnp.square(jnp.maximum(y, 0.0))
    o_ref[...] = y.astype(BF16)


def _wsmm(xm, w_all, b_all, layer, m_rows, sq_relu, name, sides=(), regroup_tiles=()):
    kdim, n = w_all.shape[1], w_all.shape[2]
    tm = next(t for t in TM_WS if m_rows % t == 0)
    tn = TN_WS
    n_j, n_i = n // tn, m_rows // tm
    assert n_i >= WS_SLABS
    side_outer = []
    in_specs = [pl.BlockSpec((tm, kdim), lambda j, i: (i, 0)),
                pl.BlockSpec((1, kdim, tn), lambda j, i: (layer, 0, j)),
                pl.BlockSpec((1, 1, tn), lambda j, i: (layer, 0, j))]
    args = [xm, w_all, b_all.reshape(DEPTH, 1, n)]
    out_specs = [pl.BlockSpec((tm, tn), lambda j, i: (i, j))]
    out_shape = [jax.ShapeDtypeStruct((xm.shape[0], n), BF16)]
    for side in sides:
        r, c = side.shape[1:]
        rows = 16
        while r % (rows * WS_SLABS) or r // (rows * WS_SLABS) > n_j:
            rows *= 2
            assert rows * WS_SLABS <= r
        n_outer = r // (rows * WS_SLABS)
        side_outer.append(n_outer)
        sidx = functools.partial(
            lambda j, i, n_outer: jnp.where(j < n_outer, j * WS_SLABS + jnp.minimum(i, WS_SLABS - 1),
                                            n_outer * WS_SLABS - 1),
            n_outer=n_outer)
        in_specs.append(pl.BlockSpec((1, rows, c), functools.partial(
            lambda j, i, sidx: (layer, sidx(j, i), 0), sidx=sidx)))
        args.append(side)
        out_specs.append(pl.BlockSpec((rows, c), functools.partial(
            lambda j, i, sidx: (sidx(j, i), 0), sidx=sidx)))
        out_shape.append(jax.ShapeDtypeStruct((r, c), BF16))
    return pl.pallas_call(
        functools.partial(_wsmm_kernel, sq_relu=sq_relu, side_outer=tuple(side_outer),
                          regroup_tiles=regroup_tiles),
        grid=(n_j, n_i),
        in_specs=in_specs,
        out_specs=out_specs,
        out_shape=out_shape,
        scratch_shapes=[pltpu.VMEM((kdim, tn), BF16), pltpu.VMEM((8, tn), F32)],
        compiler_params=_params(("parallel", "arbitrary"), 54),
        name=name,
    )(*args)


def _pool_kernel(z_ref, w_ref, s_ref, o_ref, pad_ref):
    seq = z_ref.shape[0]
    rows = seq + 2 * POOL_PAD
    zeros = jnp.zeros((POOL_PAD, POOL_GROUP), F32)
    pad_ref[0:POOL_PAD, :] = zeros
    pad_ref[POOL_PAD + seq:rows, :] = zeros
    t = lax.broadcasted_iota(jnp.int32, (seq, POOL_GROUP), 0)
    for g, win in enumerate(POOL_WINDOWS):
        cols = slice(g * POOL_GROUP, (g + 1) * POOL_GROUP)
        u = z_ref[:, cols].astype(F32)
        pad_ref[POOL_PAD:POOL_PAD + seq, :] = u
        w = pad_ref[...]
        w = pltpu.roll(w, 1, 0) + w
        width = 2
        while width < win:
            half = width // 2
            w = pltpu.roll(w, half, 0) + pltpu.roll(w, rows - half, 0)
            width *= 2
        half = win // 2
        count = jnp.minimum(t + half, seq) - jnp.maximum(t - half, 0)
        pooled = w[POOL_PAD:POOL_PAD + seq, :] / count.astype(F32) - u
        y = jnp.dot(pooled.astype(BF16), w_ref[0, g].astype(BF16), preferred_element_type=F32)
        o_ref[:, cols] = (y * s_ref[0, :, cols]).astype(BF16)


def _pool(z, seg, layer, pool_w, pool_scale, prev):
    bsz, seq, rb0 = seg
    ng = len(POOL_WINDOWS)
    return _seg_call(
        _pool_kernel, prev=prev, n_in=3,
        grid=(bsz,),
        in_specs=[pl.BlockSpec((seq, D_POOL), lambda bi: (rb0 + bi, O_POOL // D_POOL)),
                  pl.BlockSpec((1, ng, POOL_GROUP, POOL_GROUP), lambda bi: (layer, 0, 0, 0)),
                  pl.BlockSpec((1, 1, D_POOL), lambda bi: (layer, 0, 0))],
        out_specs=pl.BlockSpec((seq, D_POOL), lambda bi: (rb0 + bi, 0)),
        out_shape=jax.ShapeDtypeStruct((z.shape[0], D_POOL), BF16),
        scratch_shapes=[pltpu.VMEM((seq + 2 * POOL_PAD, POOL_GROUP), F32)],
        compiler_params=_params(("parallel",), 48),
        name="pool",
        args=[z, pool_w, pool_scale.reshape(DEPTH, 1, D_POOL)],
    )


def _conv3(u, w, b):
    seq = u.shape[0]
    t = lax.broadcasted_iota(jnp.int32, u.shape, 0)
    prev = jnp.where(t == 0, 0.0, pltpu.roll(u, 1, 0))
    nxt = jnp.where(t == seq - 1, 0.0, pltpu.roll(u, seq - 1, 0))
    return prev * w[0:1, :] + u * w[1:2, :] + nxt * w[2:3, :] + b


def _hy_pre_kernel(zv_ref, z0_ref, z1_ref, wv_ref, w0_ref, w1_ref, bv_ref, b0_ref, b1_ref,
                   uu_ref, x0_ref):
    v = _conv3(zv_ref[...].astype(F32), wv_ref[0], bv_ref[0])
    x1 = _conv3(z1_ref[...].astype(F32), w1_ref[0], b1_ref[0])
    uu_ref[0] = (v * x1).astype(BF16)
    x0_ref[0] = _conv3(z0_ref[...].astype(F32), w0_ref[0], b0_ref[0]).astype(BF16)


def _hy_pre(z, seg, layer, conv_w, conv_b):
    bsz, seq, rb0 = seg
    ct = HY_CT
    nct = D_HYENA // ct
    zoff = O_HY // ct
    zspec = [pl.BlockSpec((seq, ct), functools.partial(lambda bi, j, s: (rb0 + bi, zoff + s * nct + j), s=s))
             for s in range(3)]
    wspec = [pl.BlockSpec((1, 3, ct), functools.partial(lambda bi, j, s: (layer, 0, s * nct + j), s=s))
             for s in range(3)]
    bspec = [pl.BlockSpec((1, 1, ct), functools.partial(lambda bi, j, s: (layer, 0, s * nct + j), s=s))
             for s in range(3)]
    cb = conv_b.reshape(DEPTH, 1, 3 * D_HYENA)
    out = pl.BlockSpec((1, seq, ct), lambda bi, j: (bi, 0, j))
    return pl.pallas_call(
        _hy_pre_kernel,
        grid=(bsz, nct),
        in_specs=zspec + wspec + bspec,
        out_specs=[out, out],
        out_shape=[jax.ShapeDtypeStruct((bsz, seq, D_HYENA), BF16)] * 2,
        compiler_params=_params(("parallel", "parallel"), 48),
        name="hy_pre",
    )(z, z, z, conv_w, conv_w, conv_w, cb, cb, cb)


def _filt_kernel(zf_ref, w1_ref, b1_ref, f1_ref, w2_ref, b2_ref, f2_ref, w3_ref, b3_ref, f3_ref,
                 w4_ref, dl_ref, o_ref):
    tl = zf_ref.shape[0]
    zf = zf_ref[...]

    def dense(a, w_ref):
        return jnp.dot(a.astype(BF16), w_ref[0].astype(BF16), preferred_element_type=F32)

    hdn = jnp.sin(f1_ref[0] * (dense(zf, w1_ref) + b1_ref[0]))
    hdn = jnp.sin(f2_ref[0] * (dense(hdn, w2_ref) + b2_ref[0]))
    hdn = jnp.sin(f3_ref[0] * (dense(hdn, w3_ref) + b3_ref[0]))
    h = dense(hdn, w4_ref)
    decay = jnp.exp(-zf[:, 0:1] * jnp.abs(dl_ref[...]))
    row = lax.broadcasted_iota(jnp.int32, (tl, D_HYENA), 0) + pl.program_id(1) * tl
    o_ref[:, 0:D_HYENA] = (h[:, 0:D_HYENA] * decay).astype(BF16)
    o_ref[:, D_HYENA:] = jnp.where(row == 0, 0.0, h[:, D_HYENA:] * decay).astype(BF16)


def _filter_features(seq):
    t = jnp.linspace(0.0, 1.0, seq, dtype=F32)[:, None]
    w = 2.0 * math.pi * jnp.arange(seq, dtype=F32)[:, None] / seq
    f = jnp.linspace(1e-4, FILTER_BANDS - 1, FILTER_BANDS, dtype=F32)[None, :]
    z = jnp.concatenate([t, jnp.cos(f * w), -jnp.sin(f * w)], axis=-1)
    return jnp.pad(z, ((0, 0), (0, FEAT_PAD - FILTER_EMB)))


def _filter_deltas():
    max_decay = math.log(FILTER_DECAY_TARGET) / FILTER_FAST_PCT
    min_decay = math.log(FILTER_DECAY_TARGET) / FILTER_SLOW_PCT
    return jnp.linspace(min_decay, max_decay, D_HYENA, dtype=F32)[None, :]


def _filters(zfeat, deltas, fp):
    seq = zfeat.shape[0]
    tl = min(256, seq)
    lsel = lambda shape: pl.BlockSpec((1,) + shape, lambda l, i: (l,) + (0,) * len(shape))
    vec = lsel((1, FILTER_ORDER))
    sq = lsel((FILTER_ORDER, FILTER_ORDER))
    return pl.pallas_call(
        _filt_kernel,
        grid=(DEPTH, seq // tl),
        in_specs=[pl.BlockSpec((tl, FEAT_PAD), lambda l, i: (i, 0)),
                  lsel((FEAT_PAD, FILTER_ORDER)), vec, vec, sq, vec, vec, sq, vec, vec,
                  lsel((FILTER_ORDER, 2 * D_HYENA)),
                  pl.BlockSpec((1, D_HYENA), lambda l, i: (0, 0))],
        out_specs=pl.BlockSpec((tl, 2 * D_HYENA), lambda l, i: (i, l)),
        out_shape=jax.ShapeDtypeStruct((seq, DEPTH * 2 * D_HYENA), BF16),
        compiler_params=_params(("parallel", "parallel"), 32),
        name="hy_filter",
    )(zfeat, fp['w1'], fp['b1'], fp['f1'], fp['w2'], fp['b2'], fp['f2'], fp['w3'], fp['b3'], fp['f3'],
      fp['w4'], deltas)


def _dft_kernel(ca_ref, sa_ref, cb_ref, sb_ref, fwd_ref, inv_ref):
    ca, sa = ca_ref[0], sa_ref[0]
    cb, sb = cb_ref[...], sb_ref[...]
    cosb = ca * cb - sa * sb
    sinb = sa * cb + ca * sb
    row = lax.broadcasted_iota(jnp.int32, cosb.shape, 0) + pl.program_id(0) * cosb.shape[0]
    col = lax.broadcasted_iota(jnp.int32, cosb.shape, 1)
    alt = lambda idx: jnp.where(jnp.bitwise_and(idx, 1) == 0, 1.0, -1.0)
    fwd_ref[0] = cosb.astype(BF16)
    fwd_ref[1] = jnp.where(row == 0, alt(col), sinb).astype(BF16)
    inv_ref[...] = jnp.where(col == 0, alt(row), sinb).astype(BF16)


def _dft_matrices(seq):
    n = 2 * seq
    radix = DFT_RADIX
    t = jnp.arange(seq, dtype=jnp.int32)[None, :]
    ang = lambda f: ((f * t) % n).astype(F32) * (2.0 * math.pi / n)
    ang_a = ang(radix * jnp.arange(seq // radix, dtype=jnp.int32)[:, None])[:, None, :]
    ang_b = ang(jnp.arange(radix, dtype=jnp.int32)[:, None])
    coarse = pl.BlockSpec((1, 1, seq), lambda i: (i, 0, 0))
    fine = pl.BlockSpec((radix, seq), lambda i: (0, 0))
    return pl.pallas_call(
        _dft_kernel,
        grid=(seq // radix,),
        in_specs=[coarse, coarse, fine, fine],
        out_specs=[pl.BlockSpec((2, radix, seq), lambda i: (0, i, 0)),
                   pl.BlockSpec((radix, seq), lambda i: (i, 0))],
        out_shape=[jax.ShapeDtypeStruct((2, seq, seq), BF16), jax.ShapeDtypeStruct((seq, seq), BF16)],
        compiler_params=_params(("parallel",), 32),
        name="dft_tables",
    )(jnp.cos(ang_a), jnp.sin(ang_a), jnp.cos(ang_b), jnp.sin(ang_b))


def _mm_kernel(a_ref, b_ref, o_ref):
    o_ref[0] = jnp.dot(a_ref[0], b_ref[...], preferred_element_type=F32)


def _filter_spectrum(fwd, hcat):
    _, seq, _ = fwd.shape
    n = hcat.shape[1]
    tm = min(HY_TM, seq)
    tn = 1024
    return pl.pallas_call(
        _mm_kernel,
        grid=(2, seq // tm, n // tn),
        in_specs=[pl.BlockSpec((1, tm, seq), lambda h, i, j: (h, i, 0)),
                  pl.BlockSpec((seq, tn), lambda h, i, j: (0, j))],
        out_specs=pl.BlockSpec((1, tm, tn), lambda h, i, j: (h, i, j)),
        out_shape=jax.ShapeDtypeStruct((2, seq, n), F32),
        compiler_params=_params(("parallel", "parallel", "parallel"), 40),
        name="hy_filter_dft",
    )(fwd, hcat)


def _hy_fwd_kernel(f_ref, uu_ref, kf_ref, kb_ref, d_ref, y_ref, *, n_fft):
    tm = f_ref.shape[1]
    uu = uu_ref[0]
    sub = tm // HY_SUB
    for s in range(HY_SUB):
        rows = slice(s * sub, (s + 1) * sub)
        a = jnp.dot(f_ref[0, rows, :], uu, preferred_element_type=F32)
        b = jnp.dot(f_ref[1, rows, :], uu, preferred_element_type=F32)
        row0 = (lax.broadcasted_iota(jnp.int32, a.shape, 0) + (pl.program_id(0) * tm + s * sub)) == 0
        ka = kf_ref[0, rows, :] + kb_ref[0, rows, :] + d_ref[0]
        kb_sum = kf_ref[1, rows, :] + kb_ref[1, rows, :] + d_ref[0]
        kb_dif = kf_ref[1, rows, :] - kb_ref[1, rows, :]
        ya = jnp.where(row0, a * ka, a * ka - b * kb_dif)
        yb = jnp.where(row0, b * kb_sum, a * kb_dif + b * ka)
        wgt = jnp.where(row0, 1.0 / n_fft, 2.0 / n_fft)
        y_ref[0, 0, rows, :] = (ya * wgt).astype(BF16)
        y_ref[0, 1, rows, :] = (yb * wgt).astype(BF16)


def _hy_fwd(fwd, uu, kspec, layer, hyena_d):
    bsz, seq, _ = uu.shape
    ct = HY_FT
    nct = D_HYENA // ct
    tm = min(HY_TM, seq)
    return pl.pallas_call(
        functools.partial(_hy_fwd_kernel, n_fft=2 * seq),
        grid=(seq // tm, nct, bsz),
        in_specs=[pl.BlockSpec((2, tm, seq), lambda i, j, bi: (0, i, 0)),
                  pl.BlockSpec((1, seq, ct), lambda i, j, bi: (bi, 0, j)),
                  pl.BlockSpec((2, tm, ct), lambda i, j, bi: (0, i, 2 * nct * layer + j)),
                  pl.BlockSpec((2, tm, ct), lambda i, j, bi: (0, i, 2 * nct * layer + nct + j)),
                  pl.BlockSpec((1, 1, ct), lambda i, j, bi: (layer, 0, j))],
        out_specs=pl.BlockSpec((1, 2, tm, ct), lambda i, j, bi: (bi, 0, i, j)),
        out_shape=jax.ShapeDtypeStruct((bsz, 2, seq, D_HYENA), BF16),
        compiler_params=_params(("parallel", "parallel", "parallel"), 56),
        name="hy_fwd_dft",
    )(fwd, uu, kspec, kspec, hyena_d.reshape(DEPTH, 1, D_HYENA))


def _hy_inv_kernel(fc_ref, fs_ref, y_ref, x0_ref, o_ref):
    y = (jnp.dot(fc_ref[0], y_ref[0, 0], preferred_element_type=F32)
         + jnp.dot(fs_ref[...], y_ref[0, 1], preferred_element_type=F32))
    o_ref[...] = (y * x0_ref[0].astype(F32)).astype(BF16)


def _hy_inv(fwd, inv, yspec, x0, seg, m_total, prev):
    bsz, seq, rb0 = seg
    ct = HY_FT
    nct = D_HYENA // ct
    tm = min(HY_TM, seq)
    per = seq // tm
    return _seg_call(
        _hy_inv_kernel, prev=prev, n_in=4,
        grid=(per, nct, bsz),
        in_specs=[pl.BlockSpec((1, tm, seq), lambda i, j, bi: (0, i, 0)),
                  pl.BlockSpec((tm, seq), lambda i, j, bi: (i, 0)),
                  pl.BlockSpec((1, 2, seq, ct), lambda i, j, bi: (bi, 0, 0, j)),
                  pl.BlockSpec((1, tm, ct), lambda i, j, bi: (bi, i, j))],
        out_specs=pl.BlockSpec((tm, ct), lambda i, j, bi: ((rb0 + bi) * per + i, j)),
        out_shape=jax.ShapeDtypeStruct((m_total, D_HYENA), BF16),
        compiler_params=_params(("parallel", "parallel", "parallel"), 56),
        name="hy_inv_dft",
        args=[fwd, inv, yspec, x0],
    )


def _ret_kernel(*refs, use_rope, need_out):
    refs = list(refs)
    dec_ref, q_ref, k_ref, v_ref, g_ref = refs[:5]
    refs = refs[5:]
    if use_rope:
        cos_ref, sin_ref = refs[:2]
        refs = refs[2:]
    sf0_ref, sb0_ref = refs[:2]
    refs = refs[2:]
    if need_out:
        y_ref = refs[0]
        refs = refs[1:]
    sfo_ref, sbo_ref, qs, kst, accf, accb, st = refs

    seq = k_ref.shape[0]
    csz = RET_BLOCK
    n_chunks = seq // csz
    half = n_chunks // 2
    hd = RET_HEAD_DIM
    heads = range(RET_HPS)

    pos = lax.broadcasted_iota(jnp.int32, (csz, hd), 0).astype(F32)
    ii = lax.broadcasted_iota(jnp.int32, (csz, csz), 0)
    jj = lax.broadcasted_iota(jnp.int32, (csz, csz), 1)
    rel = (ii - jj).astype(F32)
    qdec_f, vdec_f, qdec_b, vdec_b, cdec_f, cdec_b, mask = [], [], [], [], [], [], []
    for hh in heads:
        lg = jnp.log1p(-jnp.exp(dec_ref[hh]))
        lgf = lg[0:1, :]
        lgb = lg[1:2, :]
        qdec_f.append(jnp.exp((pos + 1.0) * lgf))
        vdec_f.append(jnp.exp((csz - 1.0 - pos) * lgf))
        qdec_b.append(jnp.exp((csz - pos) * lgb))
        vdec_b.append(jnp.exp(pos * lgb))
        cdec_f.append(jnp.exp(csz * lgf))
        cdec_b.append(jnp.exp(csz * lgb))
        mask.append(jnp.where(rel >= 0, jnp.exp(jnp.maximum(rel, 0.0) * lgf[:, :csz]), 0.0)
                    + jnp.where(rel <= 0, jnp.exp(jnp.maximum(-rel, 0.0) * lgb[:, :csz]), 0.0))

    def chunk_rows(c):
        return pl.ds(pl.multiple_of(c * csz, csz), csz)

    def rope(x, rows):
        if not use_rope:
            return x
        a, b = x[:, :LANE], x[:, LANE:]
        cs, sn = cos_ref[rows, :], sin_ref[rows, :]
        return jnp.concatenate([a * cs - b * sn, b * cs + a * sn], axis=1)

    def prep(c, carry):
        rows = chunk_rows(c)
        for hh in heads:
            cols = slice(hh * hd, (hh + 1) * hd)
            kc = rope(k_ref[rows, cols].astype(F32) * (hd ** -0.5), rows)
            kst[c, hh] = kc.T.astype(BF16)
            if need_out:
                qs[rows, cols] = rope(q_ref[rows, cols].astype(F32), rows).astype(BF16)
        return carry

    lax.fori_loop(0, n_chunks, prep, 0, unroll=min(2, n_chunks))

    for hh in heads:
        st[hh, 0] = sf0_ref[0, hh]
        st[hh, 1] = sb0_ref[0, hh]

    def visit(hh, direction, c, finish):
        cols = slice(hh * hd, (hh + 1) * hd)
        rows = chunk_rows(c)
        kt = kst[c, hh]
        v = v_ref[rows, cols]
        state = st[hh, direction]
        if need_out:
            q = qs[rows, cols]
            carried = jnp.dot(q, state.astype(BF16), preferred_element_type=F32)
            if direction == 0:
                s = jnp.dot(q, kt, preferred_element_type=F32)
                o = jnp.dot((s * mask[hh]).astype(BF16), v, preferred_element_type=F32)
                o = o + qdec_f[hh] * carried
                mine, other = accf, accb
            else:
                o = qdec_b[hh] * carried
                mine, other = accb, accf
            if finish:
                o = o + other[rows, cols]
                mu = jnp.mean(o, axis=-1, keepdims=True)
                d = o - mu
                var = jnp.mean(d * d, axis=-1, keepdims=True)
                gate = _silu(g_ref[rows, cols].astype(F32))
                y_ref[rows, cols] = (gate * (d * lax.rsqrt(var + GN_EPS))).astype(BF16)
            else:
                mine[rows, cols] = o
        vdec, cdec = (vdec_f, cdec_f) if direction == 0 else (vdec_b, cdec_b)
        st[hh, direction] = state * cdec[hh] + jnp.dot(
            kt, (v.astype(F32) * vdec[hh]).astype(BF16), preferred_element_type=F32)

    def scan(finish_f, finish_b):
        def body(i, carry):
            for hh in heads:
                visit(hh, 0, i, finish_f)
                visit(hh, 1, n_chunks - 1 - i, finish_b)
            return carry
        return body

    if half:
        lax.fori_loop(0, half, scan(False, False), 0, unroll=min(2, half))
    if n_chunks % 2:
        scan(False, need_out)(half, 0)
    if half:
        lax.fori_loop(n_chunks - half, n_chunks, scan(need_out, need_out), 0, unroll=min(2, half))

    for hh in heads:
        sfo_ref[0, hh] = st[hh, 0]
        sbo_ref[0, hh] = st[hh, 1]


def _retention(z, seg, layer, dec_all, rope, s_f, s_b, need_out, prev):
    bsz, seq, rb0 = seg
    hd = RET_HEAD_DIM
    hps = RET_HPS
    wide = hps * hd
    use_rope = rope is not None
    col = lambda off: pl.BlockSpec((seq, wide), lambda bi, h: (rb0 + bi, off // wide + h))
    state = pl.BlockSpec((1, hps, hd, hd), lambda bi, h: (bi, h, 0, 0))
    in_specs = [pl.BlockSpec((None, hps, 2, hd), lambda bi, h: (layer, h, 0, 0)),
                col(O_Q), col(O_K), col(O_V), col(O_G)]
    args = [dec_all, z, z, z, z]
    if use_rope:
        in_specs += [pl.BlockSpec((seq, LANE), lambda bi, h: (0, 0))] * 2
        args += list(rope)
    in_specs += [state, state]
    args += [s_f, s_b]
    out_specs = [state, state]
    out_shape = [jax.ShapeDtypeStruct((bsz, RET_HEADS, hd, hd), F32)] * 2
    n_chunks = seq // RET_BLOCK
    scratch = [pltpu.VMEM((seq, wide), BF16), pltpu.VMEM((n_chunks, hps, hd, RET_BLOCK), BF16),
               pltpu.VMEM((seq, wide), F32), pltpu.VMEM((seq, wide), F32),
               pltpu.VMEM((hps, 2, hd, hd), F32)]
    kw = dict(grid=(bsz, RET_HEADS // hps), scratch_shapes=scratch,
              compiler_params=_params(("parallel", "parallel"), 56))
    kern = functools.partial(_ret_kernel, use_rope=use_rope, need_out=need_out)
    if not need_out:
        res = pl.pallas_call(kern, in_specs=in_specs, out_specs=out_specs, out_shape=out_shape,
                             name="retention_state", **kw)(*args)
        return None, res[0], res[1]
    out_specs = [pl.BlockSpec((seq, wide), lambda bi, h: (rb0 + bi, h))] + out_specs
    out_shape = [jax.ShapeDtypeStruct((z.shape[0], D_RET), BF16)] + out_shape
    res = _seg_call(kern, prev=prev, n_in=len(args), in_specs=in_specs, out_specs=out_specs,
                    out_shape=out_shape, name="retention", args=args, **kw)
    return res[0], res[1], res[2]


def _rope_tables(seq):
    rows = seq // GRID_W
    row = jnp.repeat(jnp.arange(rows, dtype=F32), GRID_W)
    colp = jnp.tile(jnp.arange(GRID_W, dtype=F32), rows)
    inv = ROPE_BASE ** (-jnp.arange(ROPE_PAIRS, dtype=F32) / ROPE_PAIRS)
    ang_r = row[:, None] * inv[None, :]
    ang_c = colp[:, None] * inv[None, :]
    return (jnp.concatenate([jnp.cos(ang_r), jnp.cos(ang_c)], axis=-1),
            jnp.concatenate([jnp.sin(ang_r), jnp.sin(ang_c)], axis=-1))


def _merge_kernel(ya_ref, yb_ref, yc_ref, ga_ref, gb_ref, gc_ref, pa_ref, pb_ref, pc_ref, o_ref):
    def branch(g_ref, y_ref, p_ref, rows):
        return (jax.nn.sigmoid(g_ref[rows, :].astype(F32))
                * jnp.dot(y_ref[rows, :], p_ref[...], preferred_element_type=F32))

    sub = o_ref.shape[0] // MERGE_SUB
    for s in range(MERGE_SUB):
        rows = slice(s * sub, (s + 1) * sub)
        m = (branch(ga_ref, ya_ref, pa_ref, rows) + branch(gb_ref, yb_ref, pb_ref, rows)
             + branch(gc_ref, yc_ref, pc_ref, rows))
        o_ref[rows, :] = m.astype(BF16)


def _merge(ya, yb, yc, z, p_a, p_b, p_c, m_rows):
    tm, tn = TM_MERGE, TN_WS
    goff = O_GATE // tn
    gstep = D_MODEL // tn
    act = lambda width: pl.BlockSpec((tm, width), lambda i, j: (i, 0))
    gate = lambda br: pl.BlockSpec((tm, tn), lambda i, j: (i, goff + br * gstep + j))
    wgt = lambda rows: pl.BlockSpec((rows, tn), lambda i, j: (0, j))
    return pl.pallas_call(
        _merge_kernel,
        grid=(m_rows // tm, D_MODEL // tn),
        in_specs=[act(D_POOL), act(D_HYENA), act(D_RET), gate(0), gate(1), gate(2),
                  wgt(D_POOL), wgt(D_HYENA), wgt(D_RET)],
        out_specs=pl.BlockSpec((tm, tn), lambda i, j: (i, j)),
        out_shape=jax.ShapeDtypeStruct((m_rows, D_MODEL), BF16),
        compiler_params=_params(("parallel", "arbitrary"), 48),
        name="merge",
    )(ya, yb, yc, z, z, z, p_a, p_b, p_c)


def _res_ln_kernel(*refs, emit_xm, n_k):
    a_ref, w_ref, b_ref, x_ref, gt_ref, g_ref, be_ref = refs[:7]
    refs = refs[7:]
    if emit_xm:
        sh_ref, sc_ref, xo_ref, xm_ref, acc_ref = refs
    else:
        xo_ref, acc_ref = refs
    k = pl.program_id(1)
    sub = x_ref.shape[0] // RES_SUB

    def finish(r0):
        rr = slice(r0, r0 + LN_ROWS)
        r = DEEPNORM_ALPHA * x_ref[rr, :] + gt_ref[0] * (acc_ref[rr, :] + b_ref[0])
        xn = _layer_norm(r, g_ref[0], be_ref[0])
        xo_ref[rr, :] = xn
        if emit_xm:
            xm_ref[rr, :] = (xn * (1.0 + sc_ref[0]) + sh_ref[0]).astype(BF16)

    def step(first, last):
        for s in range(RES_SUB):
            rows = slice(s * sub, (s + 1) * sub)
            part = jnp.dot(a_ref[rows, :], w_ref[...], preferred_element_type=F32)
            if first:
                acc_ref[rows, :] = part
            else:
                acc_ref[rows, :] += part
            if last:
                for r0 in range(s * sub, (s + 1) * sub, LN_ROWS):
                    finish(r0)

    if n_k == 1:
        step(True, True)
    else:
        pl.when(k == 0)(lambda: step(True, False))
        pl.when(jnp.logical_and(k > 0, k < n_k - 1))(lambda: step(False, False))
        pl.when(k == n_k - 1)(lambda: step(False, True))


def _res_ln(a, w, b_all, x, mod, layer, gate_blk, ln_g, ln_b, next_mod, m_rows, seq, bsz, name):
    kdim = a.shape[1]
    tm = TM_ROW
    tk = min(TK_ROW, kdim)
    per = seq // tm
    gidx = _mod_index(layer, per, bsz)
    vec = lambda arr: arr.reshape(DEPTH, 1, D_MODEL)
    lvec = pl.BlockSpec((1, 1, D_MODEL), lambda i, k: (layer, 0, 0))
    in_specs = [pl.BlockSpec((tm, tk), lambda i, k: (i, k)),
                pl.BlockSpec((tk, D_MODEL), lambda i, k: (k, 0)),
                lvec,
                pl.BlockSpec((tm, D_MODEL), lambda i, k: (i, 0)),
                pl.BlockSpec((1, 1, D_MODEL), lambda i, k: (gidx(i), 0, gate_blk)),
                lvec, lvec]
    args = [a, w, vec(b_all), x, mod, vec(ln_g), vec(ln_b)]
    row_out = pl.BlockSpec((tm, D_MODEL), lambda i, k: (i, 0))
    out_specs = [row_out]
    out_shape = [jax.ShapeDtypeStruct((m_rows, D_MODEL), F32)]
    if next_mod is not None:
        nl, sh_blk, sc_blk = next_mod
        nidx = _mod_index(nl, per, bsz)
        in_specs += [pl.BlockSpec((1, 1, D_MODEL), lambda i, k: (nidx(i), 0, sh_blk)),
                     pl.BlockSpec((1, 1, D_MODEL), lambda i, k: (nidx(i), 0, sc_blk))]
        args += [mod, mod]
        out_specs.append(row_out)
        out_shape.append(jax.ShapeDtypeStruct((m_rows, D_MODEL), BF16))
    res = pl.pallas_call(
        functools.partial(_res_ln_kernel, emit_xm=next_mod is not None, n_k=kdim // tk),
        grid=(m_rows // tm, kdim // tk),
        in_specs=in_specs,
        out_specs=out_specs,
        out_shape=out_shape,
        scratch_shapes=[pltpu.VMEM((tm, D_MODEL), F32)],
        compiler_params=_params(("parallel", "arbitrary"), 56),
        name=name,
    )(*args)
    return (res[0], res[1]) if next_mod is not None else (res[0], None)


def kernel(x, c, ctx, c_ctx, w_ada, b_ada, w_in, b_in, conv_w, conv_b, pool_w, pool_scale, filt_w1, filt_b1, filt_f1, filt_w2, filt_b2, filt_f2, filt_w3, filt_b3, filt_f3, filt_w4, hyena_d, ret_decay, p_a, p_b, p_c, w_o, b_o, ln1_g, ln1_b, w_mlp1, b_mlp1, w_mlp2, b_mlp2, ln2_g, ln2_b):
    bsz, seq, _ = x.shape
    ctx_len = ctx.shape[1]
    assert x.shape == (bsz, seq, D_MODEL) and ctx.shape == (bsz, ctx_len, D_MODEL)
    assert seq % RET_BLOCK == 0 and ctx_len % RET_BLOCK == 0 and seq % GRID_W == 0
    assert bsz + 1 <= ADA_ROWS and seq % ctx_len == 0
    n_lat, n_ctx = bsz * seq, bsz * ctx_len
    m_total = n_lat + n_ctx
    assert seq % TM_MERGE == 0 and n_ctx % TM_MERGE == 0 and seq % TM_ROW == 0 and n_ctx % TM_ROW == 0
    assert O_Q % TN_WS == 0 and O_V % TN_WS == 0
    seg_x = (bsz, seq, 0)
    seg_c = (bsz, ctx_len, n_lat // ctx_len)

    cvec = jnp.concatenate([c, c_ctx[None, :], jnp.zeros((ADA_ROWS - bsz - 1, D_MODEL), F32)], axis=0)
    mod = _ada(cvec, w_ada, b_ada).reshape(DEPTH * ADA_ROWS, 1, 6 * D_MODEL)

    deltas = _filter_deltas()
    dft_x, dft_c = _dft_matrices(seq), _dft_matrices(ctx_len)
    zfeat_x, zfeat_c = _filter_features(seq), _filter_features(ctx_len)
    rope = _rope_tables(seq)
    zero_state = jnp.zeros((bsz, RET_HEADS, RET_HEAD_DIM, RET_HEAD_DIM), F32)

    row3 = lambda a: a.reshape(DEPTH, 1, -1)
    fp = {'w1': jnp.pad(filt_w1, ((0, 0), (0, FEAT_PAD - FILTER_EMB), (0, 0))),
          'b1': row3(filt_b1), 'f1': row3(filt_f1), 'w2': filt_w2, 'b2': row3(filt_b2), 'f2': row3(filt_f2),
          'w3': filt_w3, 'b3': row3(filt_b3), 'f3': row3(filt_f3), 'w4': filt_w4}
    dec_all = jnp.broadcast_to(jnp.swapaxes(ret_decay, 1, 2)[:, :, :, None],
                               (DEPTH, RET_HEADS, 2, RET_HEAD_DIM))
    kspec_x = _filter_spectrum(dft_x[0], _filters(zfeat_x, deltas, fp))
    kspec_c = _filter_spectrum(dft_c[0], _filters(zfeat_c, deltas, fp))
    xs, xm = _mod0(x.reshape(n_lat, D_MODEL), ctx.reshape(n_ctx, D_MODEL), mod, seq, bsz)

    def hyena(z, seg, l, dft, kspec, prev):
        fwd, inv = dft
        uu, x0 = _hy_pre(z, seg, l, conv_w, conv_b)
        return _hy_inv(fwd, inv, _hy_fwd(fwd, uu, kspec, l, hyena_d), x0, seg, m_total, prev)

    ya = jnp.zeros((m_total, D_POOL), BF16)
    yb = jnp.zeros((m_total, D_HYENA), BF16)
    yc = jnp.zeros((m_total, D_RET), BF16)
    qk_tiles = tuple(range(O_Q // TN_WS, O_V // TN_WS))
    for l in range(DEPTH):
        last = l == DEPTH - 1
        rows = n_lat if last else m_total
        z, wo_b, pa_b, pb_b, pc_b = _wsmm(xm, w_in, b_in, l, m_total, False, "in_proj",
                                          sides=(w_o, p_a, p_b, p_c), regroup_tiles=qk_tiles)
        if not last:
            ya = _pool(z, seg_c, l, pool_w, pool_scale, ya)
            yb = hyena(z, seg_c, l, dft_c, kspec_c, yb)
            yc, s_f, s_b = _retention(z, seg_c, l, dec_all, None, zero_state, zero_state, True, yc)
        else:
            _, s_f, s_b = _retention(z, seg_c, l, dec_all, None, zero_state, zero_state, False, None)
        ya = _pool(z, seg_x, l, pool_w, pool_scale, ya)
        yb = hyena(z, seg_x, l, dft_x, kspec_x, yb)
        yc, _, _ = _retention(z, seg_x, l, dec_all, rope, s_f, s_b, True, yc)
        merged = _merge(ya, yb, yc, z, pa_b, pb_b, pc_b, rows)
        xs, xm = _res_ln(merged, wo_b, b_o, xs, mod, l, 2, ln1_g, ln1_b, (l, 3, 4), rows, seq, bsz,
                         "out_proj_ln1")
        hid, w2_b = _wsmm(xm, w_mlp1, b_mlp1, l, rows, True, "mlp_up", sides=(w_mlp2,))
        xs, xm = _res_ln(hid, w2_b, b_mlp2, xs, mod, l, 5, ln2_g, ln2_b,
                         None if last else (l + 1, 0, 1), rows, seq, bsz, "mlp_down_ln2")
    return xs.reshape(bsz, seq, D_MODEL)
```

```python
import functools
import math

import jax
import jax.numpy as jnp
from jax import lax
from jax.experimental import pallas as pl
from jax.experimental.pallas import tpu as pltpu

F32 = jnp.float32
BF16 = jnp.bfloat16

D_MODEL = 2048
DEPTH = 4
GRID_W = 64
D_POOL = D_MODEL // 4
POOL_WINDOWS = (2, 4, 8, 16)
POOL_GROUP = D_POOL // len(POOL_WINDOWS)
D_HYENA = D_MODEL // 4
FILTER_EMB = 33
FILTER_BANDS = (FILTER_EMB - 1) // 2
FILTER_ORDER = 64
FILTER_DECAY_TARGET = 1e-2
FILTER_FAST_PCT = 0.3
FILTER_SLOW_PCT = 1.5
RET_HEAD_DIM = 256
D_RET = D_MODEL // 2
RET_HEADS = D_RET // RET_HEAD_DIM
RET_CHUNK = 128
ROPE_BASE = 10000.0
ROPE_PAIRS = RET_HEAD_DIM // 4
N_BRANCH = 3
D_FF = 4 * D_MODEL
LN_EPS = 1e-5
GN_EPS = 1e-6
DEEPNORM_ALPHA = (2 * DEPTH) ** 0.25
O_POOL = 0
O_HY = O_POOL + D_POOL
O_Q = O_HY + 3 * D_HYENA
O_K = O_Q + D_RET
O_V = O_K + D_RET
O_G = O_V + D_RET
O_GATE = O_G + D_RET
D_IN = O_GATE + N_BRANCH * D_MODEL

LANE = 128
FEAT_PAD = LANE
POOL_PAD = 16
ADA_ROWS = 8
MIB = 1024 * 1024

TM_WS = (1536, 1024)
TM_MERGE = 1024
TN_WS = 1024
WS_SLABS = 4
TM_ROW = 512
TK_ROW = 2048
RES_SUB = 4
MERGE_SUB = 2
LN_ROWS = 16
HY_CT = 256
HY_FT = 512
HY_TM = 1024
HY_SUB = 2
DFT_RADIX = 64
RET_HPS = 2
RET_BLOCK = 256


def _params(semantics, vmem_mib):
    return pltpu.CompilerParams(dimension_semantics=semantics, vmem_limit_bytes=vmem_mib * MIB)


def _silu(v):
    return v * jax.nn.sigmoid(v)


def _layer_norm(r, g, b):
    mu = jnp.mean(r, axis=-1, keepdims=True)
    d = r - mu
    var = jnp.mean(d * d, axis=-1, keepdims=True)
    return d * lax.rsqrt(var + LN_EPS) * g + b


def _skip_ref(kern, idx):
    def wrapped(*refs):
        return kern(*refs[:idx], *refs[idx + 1:])
    return wrapped


def _seg_call(kern, *, prev, n_in, **kw):
    in_specs = list(kw.pop("in_specs")) + [pl.BlockSpec(memory_space=pl.ANY)]
    args = list(kw.pop("args")) + [prev]
    return pl.pallas_call(_skip_ref(kern, n_in), in_specs=in_specs,
                          input_output_aliases={n_in: 0}, **kw)(*args)


def _mod_index(layer, per_batch_tiles, bsz):
    return lambda i: layer * ADA_ROWS + jnp.minimum(i // per_batch_tiles, bsz)


def _ada_kernel(c_ref, w_ref, b_ref, o_ref):
    s = _silu(c_ref[...]).astype(BF16)
    o_ref[0] = jnp.dot(s, w_ref[0].astype(BF16), preferred_element_type=F32) + b_ref[0]


def _ada(cvec, w_ada, b_ada):
    tn = 1024
    n = w_ada.shape[2]
    return pl.pallas_call(
        _ada_kernel,
        grid=(DEPTH, n // tn),
        in_specs=[
            pl.BlockSpec((ADA_ROWS, D_MODEL), lambda l, j: (0, 0)),
            pl.BlockSpec((1, D_MODEL, tn), lambda l, j: (l, 0, j)),
            pl.BlockSpec((1, 1, tn), lambda l, j: (l, 0, j)),
        ],
        out_specs=pl.BlockSpec((1, ADA_ROWS, tn), lambda l, j: (l, 0, j)),
        out_shape=jax.ShapeDtypeStruct((DEPTH, ADA_ROWS, n), F32),
        compiler_params=_params(("parallel", "parallel"), 40),
        name="ada",
    )(cvec, w_ada, b_ada.reshape(DEPTH, 1, n))


def _mod0_kernel(x_ref, c_ref, sh_ref, sc_ref, xo_ref, xm_ref, *, n_lat_tiles):
    def emit(v):
        xo_ref[...] = v
        xm_ref[...] = (v * (1.0 + sc_ref[0]) + sh_ref[0]).astype(BF16)

    i = pl.program_id(0)

    @pl.when(i < n_lat_tiles)
    def _():
        emit(x_ref[...])

    @pl.when(i >= n_lat_tiles)
    def _():
        emit(c_ref[...])


def _mod0(x2d, c2d, mod, seq, bsz):
    n_lat, n_ctx = x2d.shape[0], c2d.shape[0]
    tm = TM_ROW
    nl = n_lat // tm
    midx = _mod_index(0, seq // tm, bsz)
    return pl.pallas_call(
        functools.partial(_mod0_kernel, n_lat_tiles=nl),
        grid=((n_lat + n_ctx) // tm,),
        in_specs=[pl.BlockSpec((tm, D_MODEL), lambda i: (jnp.minimum(i, nl - 1), 0)),
                  pl.BlockSpec((tm, D_MODEL), lambda i: (jnp.maximum(i - nl, 0), 0)),
                  pl.BlockSpec((1, 1, D_MODEL), lambda i: (midx(i), 0, 0)),
                  pl.BlockSpec((1, 1, D_MODEL), lambda i: (midx(i), 0, 1))],
        out_specs=[pl.BlockSpec((tm, D_MODEL), lambda i: (i, 0)),
                   pl.BlockSpec((tm, D_MODEL), lambda i: (i, 0))],
        out_shape=[jax.ShapeDtypeStruct((n_lat + n_ctx, D_MODEL), F32),
                   jax.ShapeDtypeStruct((n_lat + n_ctx, D_MODEL), BF16)],
        compiler_params=_params(("parallel",), 40),
        name="assemble_modulate",
    )(x2d, c2d, mod, mod)


def _regroup(a, b):
    lo = lax.broadcasted_iota(jnp.int32, a.shape, 1) < LANE // 2
    return (jnp.where(lo, a, pltpu.roll(b, LANE // 2, 1)),
            jnp.where(lo, pltpu.roll(a, LANE // 2, 1), b))


def _wsmm_kernel(*refs, sq_relu, side_outer, regroup_tiles):
    n_side = len(side_outer)
    x_ref, w_ref, b_ref = refs[:3]
    s_refs = refs[3:3 + n_side]
    o_ref = refs[3 + n_side]
    so_refs = refs[4 + n_side:4 + 2 * n_side]
    wb_ref, bb_ref = refs[4 + 2 * n_side:]
    j = pl.program_id(0)
    i = pl.program_id(1)
    first = i == 0
    tn = wb_ref.shape[1]
    for s_ref, so_ref, n_outer in zip(s_refs, so_refs, side_outer):
        @pl.when(jnp.logical_and(j < n_outer, i < WS_SLABS))
        def _(s_ref=s_ref, so_ref=so_ref):
            so_ref[...] = s_ref[0].astype(BF16)

    def plain():
        wb_ref[...] = w_ref[0].astype(BF16)
        bb_ref[...] = jnp.broadcast_to(b_ref[0], bb_ref.shape)

    def regrouped():
        b8 = jnp.broadcast_to(b_ref[0], bb_ref.shape)
        for c0 in range(0, tn, 2 * LANE):
            lo, hi = slice(c0, c0 + LANE), slice(c0 + LANE, c0 + 2 * LANE)
            wa, wc = _regroup(w_ref[0, :, lo], w_ref[0, :, hi])
            wb_ref[:, lo] = wa.astype(BF16)
            wb_ref[:, hi] = wc.astype(BF16)
            ba, bc = _regroup(b8[:, lo], b8[:, hi])
            bb_ref[:, lo] = ba
            bb_ref[:, hi] = bc

    if regroup_tiles:
        hit = functools.reduce(jnp.logical_or, [j == t for t in regroup_tiles])
        pl.when(jnp.logical_and(first, hit))(regrouped)
        pl.when(jnp.logical_and(first, jnp.logical_not(hit)))(plain)
    else:
        pl.when(first)(plain)

    y = jnp.dot(x_ref[...], wb_ref[...], preferred_element_type=F32) + bb_ref[0:1, :]
    if sq_relu:
        y = jnp.square(jnp.maximum(y, 0.0))
    o_ref[...] = y.astype(BF16)


def _wsmm(xm, w_all, b_all, layer, m_rows, sq_relu, name, sides=(), regroup_tiles=()):
    kdim, n = w_all.shape[1], w_all.shape[2]
    tm = next(t for t in TM_WS if m_rows % t == 0)
    tn = TN_WS
    n_j, n_i = n // tn, m_rows // tm
    assert n_i >= WS_SLABS
    side_outer = []
    in_specs = [pl.BlockSpec((tm, kdim), lambda j, i: (i, 0)),
                pl.BlockSpec((1, kdim, tn), lambda j, i: (layer, 0, j)),
                pl.BlockSpec((1, 1, tn), lambda j, i: (layer, 0, j))]
    args = [xm, w_all, b_all.reshape(DEPTH, 1, n)]
    out_specs = [pl.BlockSpec((tm, tn), lambda j, i: (i, j))]
    out_shape = [jax.ShapeDtypeStruct((xm.shape[0], n), BF16)]
    for side in sides:
        r, c = side.shape[1:]
        rows = 16
        while r % (rows * WS_SLABS) or r // (rows * WS_SLABS) > n_j:
            rows *= 2
            assert rows * WS_SLABS <= r
        n_outer = r // (rows * WS_SLABS)
        side_outer.append(n_outer)
        sidx = functools.partial(
            lambda j, i, n_outer: jnp.where(j < n_outer, j * WS_SLABS + jnp.minimum(i, WS_SLABS - 1),
                                            n_outer * WS_SLABS - 1),
            n_outer=n_outer)
        in_specs.append(pl.BlockSpec((1, rows, c), functools.partial(
            lambda j, i, sidx: (layer, sidx(j, i), 0), sidx=sidx)))
        args.append(side)
        out_specs.append(pl.BlockSpec((rows, c), functools.partial(
            lambda j, i, sidx: (sidx(j, i), 0), sidx=sidx)))
        out_shape.append(jax.ShapeDtypeStruct((r, c), BF16))
    return pl.pallas_call(
        functools.partial(_wsmm_kernel, sq_relu=sq_relu, side_outer=tuple(side_outer),
                          regroup_tiles=regroup_tiles),
        grid=(n_j, n_i),
        in_specs=in_specs,
        out_specs=out_specs,
        out_shape=out_shape,
        scratch_shapes=[pltpu.VMEM((kdim, tn), BF16), pltpu.VMEM((8, tn), F32)],
        compiler_params=_params(("parallel", "arbitrary"), 54),
        name=name,
    )(*args)


def _pool_kernel(z_ref, w_ref, s_ref, o_ref, pad_ref):
    seq = z_ref.shape[0]
    rows = seq + 2 * POOL_PAD
    zeros = jnp.zeros((POOL_PAD, POOL_GROUP), F32)
    pad_ref[0:POOL_PAD, :] = zeros
    pad_ref[POOL_PAD + seq:rows, :] = zeros
    t = lax.broadcasted_iota(jnp.int32, (seq, POOL_GROUP), 0)
    for g, win in enumerate(POOL_WINDOWS):
        cols = slice(g * POOL_GROUP, (g + 1) * POOL_GROUP)
        u = z_ref[:, cols].astype(F32)
        pad_ref[POOL_PAD:POOL_PAD + seq, :] = u
        w = pad_ref[...]
        w = pltpu.roll(w, 1, 0) + w
        width = 2
        while width < win:
            half = width // 2
            w = pltpu.roll(w, half, 0) + pltpu.roll(w, rows - half, 0)
            width *= 2
        half = win // 2
        count = jnp.minimum(t + half, seq) - jnp.maximum(t - half, 0)
        pooled = w[POOL_PAD:POOL_PAD + seq, :] / count.astype(F32) - u
        y = jnp.dot(pooled.astype(BF16), w_ref[0, g].astype(BF16), preferred_element_type=F32)
        o_ref[:, cols] = (y * s_ref[0, :, cols]).astype(BF16)


def _pool(z, seg, layer, pool_w, pool_scale, prev):
    bsz, seq, rb0 = seg
    ng = len(POOL_WINDOWS)
    return _seg_call(
        _pool_kernel, prev=prev, n_in=3,
        grid=(bsz,),
        in_specs=[pl.BlockSpec((seq, D_POOL), lambda bi: (rb0 + bi, O_POOL // D_POOL)),
                  pl.BlockSpec((1, ng, POOL_GROUP, POOL_GROUP), lambda bi: (layer, 0, 0, 0)),
                  pl.BlockSpec((1, 1, D_POOL), lambda bi: (layer, 0, 0))],
        out_specs=pl.BlockSpec((seq, D_POOL), lambda bi: (rb0 + bi, 0)),
        out_shape=jax.ShapeDtypeStruct((z.shape[0], D_POOL), BF16),
        scratch_shapes=[pltpu.VMEM((seq + 2 * POOL_PAD, POOL_GROUP), F32)],
        compiler_params=_params(("parallel",), 48),
        name="pool",
        args=[z, pool_w, pool_scale.reshape(DEPTH, 1, D_POOL)],
    )


def _conv3(u, w, b):
    seq = u.shape[0]
    t = lax.broadcasted_iota(jnp.int32, u.shape, 0)
    prev = jnp.where(t == 0, 0.0, pltpu.roll(u, 1, 0))
    nxt = jnp.where(t == seq - 1, 0.0, pltpu.roll(u, seq - 1, 0))
    return prev * w[0:1, :] + u * w[1:2, :] + nxt * w[2:3, :] + b


def _hy_pre_kernel(zv_ref, z0_ref, z1_ref, wv_ref, w0_ref, w1_ref, bv_ref, b0_ref, b1_ref,
                   uu_ref, x0_ref):
    v = _conv3(zv_ref[...].astype(F32), wv_ref[0], bv_ref[0])
    x1 = _conv3(z1_ref[...].astype(F32), w1_ref[0], b1_ref[0])
    uu_ref[0] = (v * x1).astype(BF16)
    x0_ref[0] = _conv3(z0_ref[...].astype(F32), w0_ref[0], b0_ref[0]).astype(BF16)


def _hy_pre(z, seg, layer, conv_w, conv_b):
    bsz, seq, rb0 = seg
    ct = HY_CT
    nct = D_HYENA // ct
    zoff = O_HY // ct
    zspec = [pl.BlockSpec((seq, ct), functools.partial(lambda bi, j, s: (rb0 + bi, zoff + s * nct + j), s=s))
             for s in range(3)]
    wspec = [pl.BlockSpec((1, 3, ct), functools.partial(lambda bi, j, s: (layer, 0, s * nct + j), s=s))
             for s in range(3)]
    bspec = [pl.BlockSpec((1, 1, ct), functools.partial(lambda bi, j, s: (layer, 0, s * nct + j), s=s))
             for s in range(3)]
    cb = conv_b.reshape(DEPTH, 1, 3 * D_HYENA)
    out = pl.BlockSpec((1, seq, ct), lambda bi, j: (bi, 0, j))
    return pl.pallas_call(
        _hy_pre_kernel,
        grid=(bsz, nct),
        in_specs=zspec + wspec + bspec,
        out_specs=[out, out],
        out_shape=[jax.ShapeDtypeStruct((bsz, seq, D_HYENA), BF16)] * 2,
        compiler_params=_params(("parallel", "parallel"), 48),
        name="hy_pre",
    )(z, z, z, conv_w, conv_w, conv_w, cb, cb, cb)


def _filt_kernel(zf_ref, w1_ref, b1_ref, f1_ref, w2_ref, b2_ref, f2_ref, w3_ref, b3_ref, f3_ref,
                 w4_ref, dl_ref, o_ref):
    tl = zf_ref.shape[0]
    zf = zf_ref[...]

    def dense(a, w_ref):
        return jnp.dot(a.astype(BF16), w_ref[0].astype(BF16), preferred_element_type=F32)

    hdn = jnp.sin(f1_ref[0] * (dense(zf, w1_ref) + b1_ref[0]))
    hdn = jnp.sin(f2_ref[0] * (dense(hdn, w2_ref) + b2_ref[0]))
    hdn = jnp.sin(f3_ref[0] * (dense(hdn, w3_ref) + b3_ref[0]))
    h = dense(hdn, w4_ref)
    decay = jnp.exp(-zf[:, 0:1] * jnp.abs(dl_ref[...]))
    row = lax.broadcasted_iota(jnp.int32, (tl, D_HYENA), 0) + pl.program_id(1) * tl
    o_ref[:, 0:D_HYENA] = (h[:, 0:D_HYENA] * decay).astype(BF16)
    o_ref[:, D_HYENA:] = jnp.where(row == 0, 0.0, h[:, D_HYENA:] * decay).astype(BF16)


def _filter_features(seq):
    t = jnp.linspace(0.0, 1.0, seq, dtype=F32)[:, None]
    w = 2.0 * math.pi * jnp.arange(seq, dtype=F32)[:, None] / seq
    f = jnp.linspace(1e-4, FILTER_BANDS - 1, FILTER_BANDS, dtype=F32)[None, :]
    z = jnp.concatenate([t, jnp.cos(f * w), -jnp.sin(f * w)], axis=-1)
    return jnp.pad(z, ((0, 0), (0, FEAT_PAD - FILTER_EMB)))


def _filter_deltas():
    max_decay = math.log(FILTER_DECAY_TARGET) / FILTER_FAST_PCT
    min_decay = math.log(FILTER_DECAY_TARGET) / FILTER_SLOW_PCT
    return jnp.linspace(min_decay, max_decay, D_HYENA, dtype=F32)[None, :]


def _filters(zfeat, deltas, fp):
    seq = zfeat.shape[0]
    tl = min(256, seq)
    lsel = lambda shape: pl.BlockSpec((1,) + shape, lambda l, i: (l,) + (0,) * len(shape))
    vec = lsel((1, FILTER_ORDER))
    sq = lsel((FILTER_ORDER, FILTER_ORDER))
    return pl.pallas_call(
        _filt_kernel,
        grid=(DEPTH, seq // tl),
        in_specs=[pl.BlockSpec((tl, FEAT_PAD), lambda l, i: (i, 0)),
                  lsel((FEAT_PAD, FILTER_ORDER)), vec, vec, sq, vec, vec, sq, vec, vec,
                  lsel((FILTER_ORDER, 2 * D_HYENA)),
                  pl.BlockSpec((1, D_HYENA), lambda l, i: (0, 0))],
        out_specs=pl.BlockSpec((tl, 2 * D_HYENA), lambda l, i: (i, l)),
        out_shape=jax.ShapeDtypeStruct((seq, DEPTH * 2 * D_HYENA), BF16),
        compiler_params=_params(("parallel", "parallel"), 32),
        name="hy_filter",
    )(zfeat, fp['w1'], fp['b1'], fp['f1'], fp['w2'], fp['b2'], fp['f2'], fp['w3'], fp['b3'], fp['f3'],
      fp['w4'], deltas)


def _dft_kernel(ca_ref, sa_ref, cb_ref, sb_ref, fwd_ref, inv_ref):
    ca, sa = ca_ref[0], sa_ref[0]
    cb, sb = cb_ref[...], sb_ref[...]
    cosb = ca * cb - sa * sb
    sinb = sa * cb + ca * sb
    row = lax.broadcasted_iota(jnp.int32, cosb.shape, 0) + pl.program_id(0) * cosb.shape[0]
    col = lax.broadcasted_iota(jnp.int32, cosb.shape, 1)
    alt = lambda idx: jnp.where(jnp.bitwise_and(idx, 1) == 0, 1.0, -1.0)
    fwd_ref[0] = cosb.astype(BF16)
    fwd_ref[1] = jnp.where(row == 0, alt(col), sinb).astype(BF16)
    inv_ref[...] = jnp.where(col == 0, alt(row), sinb).astype(BF16)


def _dft_matrices(seq):
    n = 2 * seq
    radix = DFT_RADIX
    t = jnp.arange(seq, dtype=jnp.int32)[None, :]
    ang = lambda f: ((f * t) % n).astype(F32) * (2.0 * math.pi / n)
    ang_a = ang(radix * jnp.arange(seq // radix, dtype=jnp.int32)[:, None])[:, None, :]
    ang_b = ang(jnp.arange(radix, dtype=jnp.int32)[:, None])
    coarse = pl.BlockSpec((1, 1, seq), lambda i: (i, 0, 0))
    fine = pl.BlockSpec((radix, seq), lambda i: (0, 0))
    return pl.pallas_call(
        _dft_kernel,
        grid=(seq // radix,),
        in_specs=[coarse, coarse, fine, fine],
        out_specs=[pl.BlockSpec((2, radix, seq), lambda i: (0, i, 0)),
                   pl.BlockSpec((radix, seq), lambda i: (i, 0))],
        out_shape=[jax.ShapeDtypeStruct((2, seq, seq), BF16), jax.ShapeDtypeStruct((seq, seq), BF16)],
        compiler_params=_params(("parallel",), 32),
        name="dft_tables",
    )(jnp.cos(ang_a), jnp.sin(ang_a), jnp.cos(ang_b), jnp.sin(ang_b))


def _mm_kernel(a_ref, b_ref, o_ref):
    o_ref[0] = jnp.dot(a_ref[0], b_ref[...], preferred_element_type=F32)


def _filter_spectrum(fwd, hcat):
    _, seq, _ = fwd.shape
    n = hcat.shape[1]
    tm = min(HY_TM, seq)
    tn = 1024
    return pl.pallas_call(
        _mm_kernel,
        grid=(2, seq // tm, n // tn),
        in_specs=[pl.BlockSpec((1, tm, seq), lambda h, i, j: (h, i, 0)),
                  pl.BlockSpec((seq, tn), lambda h, i, j: (0, j))],
        out_specs=pl.BlockSpec((1, tm, tn), lambda h, i, j: (h, i, j)),
        out_shape=jax.ShapeDtypeStruct((2, seq, n), F32),
        compiler_params=_params(("parallel", "parallel", "parallel"), 40),
        name="hy_filter_dft",
    )(fwd, hcat)


def _hy_fwd_kernel(f_ref, uu_ref, kf_ref, kb_ref, d_ref, y_ref, *, n_fft):
    tm = f_ref.shape[1]
    uu = uu_ref[0]
    sub = tm // HY_SUB
    for s in range(HY_SUB):
        rows = slice(s * sub, (s + 1) * sub)
        a = jnp.dot(f_ref[0, rows, :], uu, preferred_element_type=F32)
        b = jnp.dot(f_ref[1, rows, :], uu, preferred_element_type=F32)
        row0 = (lax.broadcasted_iota(jnp.int32, a.shape, 0) + (pl.program_id(0) * tm + s * sub)) == 0
        ka = kf_ref[0, rows, :] + kb_ref[0, rows, :] + d_ref[0]
        kb_sum = kf_ref[1, rows, :] + kb_ref[1, rows, :] + d_ref[0]
        kb_dif = kf_ref[1, rows, :] - kb_ref[1, rows, :]
        ya = jnp.where(row0, a * ka, a * ka - b * kb_dif)
        yb = jnp.where(row0, b * kb_sum, a * kb_dif + b * ka)
        wgt = jnp.where(row0, 1.0 / n_fft, 2.0 / n_fft)
        y_ref[0, 0, rows, :] = (ya * wgt).astype(BF16)
        y_ref[0, 1, rows, :] = (yb * wgt).astype(BF16)


def _hy_fwd(fwd, uu, kspec, layer, hyena_d):
    bsz, seq, _ = uu.shape
    ct = HY_FT
    nct = D_HYENA // ct
    tm = min(HY_TM, seq)
    return pl.pallas_call(
        functools.partial(_hy_fwd_kernel, n_fft=2 * seq),
        grid=(seq // tm, nct, bsz),
        in_specs=[pl.BlockSpec((2, tm, seq), lambda i, j, bi: (0, i, 0)),
                  pl.BlockSpec((1, seq, ct), lambda i, j, bi: (bi, 0, j)),
                  pl.BlockSpec((2, tm, ct), lambda i, j, bi: (0, i, 2 * nct * layer + j)),
                  pl.BlockSpec((2, tm, ct), lambda i, j, bi: (0, i, 2 * nct * layer + nct + j)),
                  pl.BlockSpec((1, 1, ct), lambda i, j, bi: (layer, 0, j))],
        out_specs=pl.BlockSpec((1, 2, tm, ct), lambda i, j, bi: (bi, 0, i, j)),
        out_shape=jax.ShapeDtypeStruct((bsz, 2, seq, D_HYENA), BF16),
        compiler_params=_params(("parallel", "parallel", "parallel"), 56),
        name="hy_fwd_dft",
    )(fwd, uu, kspec, kspec, hyena_d.reshape(DEPTH, 1, D_HYENA))


def _hy_inv_kernel(fc_ref, fs_ref, y_ref, x0_ref, o_ref):
    y = (jnp.dot(fc_ref[0], y_ref[0, 0], preferred_element_type=F32)
         + jnp.dot(fs_ref[...], y_ref[0, 1], preferred_element_type=F32))
    o_ref[...] = (y * x0_ref[0].astype(F32)).astype(BF16)


def _hy_inv(fwd, inv, yspec, x0, seg, m_total, prev):
    bsz, seq, rb0 = seg
    ct = HY_FT
    nct = D_HYENA // ct
    tm = min(HY_TM, seq)
    per = seq // tm
    return _seg_call(
        _hy_inv_kernel, prev=prev, n_in=4,
        grid=(per, nct, bsz),
        in_specs=[pl.BlockSpec((1, tm, seq), lambda i, j, bi: (0, i, 0)),
                  pl.BlockSpec((tm, seq), lambda i, j, bi: (i, 0)),
                  pl.BlockSpec((1, 2, seq, ct), lambda i, j, bi: (bi, 0, 0, j)),
                  pl.BlockSpec((1, tm, ct), lambda i, j, bi: (bi, i, j))],
        out_specs=pl.BlockSpec((tm, ct), lambda i, j, bi: ((rb0 + bi) * per + i, j)),
        out_shape=jax.ShapeDtypeStruct((m_total, D_HYENA), BF16),
        compiler_params=_params(("parallel", "parallel", "parallel"), 56),
        name="hy_inv_dft",
        args=[fwd, inv, yspec, x0],
    )


def _ret_kernel(*refs, use_rope, need_out):
    refs = list(refs)
    dec_ref, q_ref, k_ref, v_ref, g_ref = refs[:5]
    refs = refs[5:]
    if use_rope:
        cos_ref, sin_ref = refs[:2]
        refs = refs[2:]
    sf0_ref, sb0_ref = refs[:2]
    refs = refs[2:]
    if need_out:
        y_ref = refs[0]
        refs = refs[1:]
    sfo_ref, sbo_ref, qs, kst, accf, accb, st = refs

    seq = k_ref.shape[0]
    csz = RET_BLOCK
    n_chunks = seq // csz
    half = n_chunks // 2
    hd = RET_HEAD_DIM
    heads = range(RET_HPS)

    pos = lax.broadcasted_iota(jnp.int32, (csz, hd), 0).astype(F32)
    ii = lax.broadcasted_iota(jnp.int32, (csz, csz), 0)
    jj = lax.broadcasted_iota(jnp.int32, (csz, csz), 1)
    rel = (ii - jj).astype(F32)
    qdec_f, vdec_f, qdec_b, vdec_b, cdec_f, cdec_b, mask = [], [], [], [], [], [], []
    for hh in heads:
        lg = jnp.log1p(-jnp.exp(dec_ref[hh]))
        lgf = lg[0:1, :]
        lgb = lg[1:2, :]
        qdec_f.append(jnp.exp((pos + 1.0) * lgf))
        vdec_f.append(jnp.exp((csz - 1.0 - pos) * lgf))
        qdec_b.append(jnp.exp((csz - pos) * lgb))
        vdec_b.append(jnp.exp(pos * lgb))
        cdec_f.append(jnp.exp(csz * lgf))
        cdec_b.append(jnp.exp(csz * lgb))
        mask.append(jnp.where(rel >= 0, jnp.exp(jnp.maximum(rel, 0.0) * lgf[:, :csz]), 0.0)
                    + jnp.where(rel <= 0, jnp.exp(jnp.maximum(-rel, 0.0) * lgb[:, :csz]), 0.0))

    def chunk_rows(c):
        return pl.ds(pl.multiple_of(c * csz, csz), csz)

    def rope(x, rows):
        if not use_rope:
            return x
        a, b = x[:, :LANE], x[:, LANE:]
        cs, sn = cos_ref[rows, :], sin_ref[rows, :]
        return jnp.concatenate([a * cs - b * sn, b * cs + a * sn], axis=1)

    def prep(c, carry):
        rows = chunk_rows(c)
        for hh in heads:
            cols = slice(hh * hd, (hh + 1) * hd)
            kc = rope(k_ref[rows, cols].astype(F32) * (hd ** -0.5), rows)
            kst[c, hh] = kc.T.astype(BF16)
            if need_out:
                qs[rows, cols] = rope(q_ref[rows, cols].astype(F32), rows).astype(BF16)
        return carry

    lax.fori_loop(0, n_chunks, prep, 0, unroll=min(2, n_chunks))

    for hh in heads:
        st[hh, 0] = sf0_ref[0, hh]
        st[hh, 1] = sb0_ref[0, hh]

    def visit(hh, direction, c, finish):
        cols = slice(hh * hd, (hh + 1) * hd)
        rows = chunk_rows(c)
        kt = kst[c, hh]
        v = v_ref[rows, cols]
        state = st[hh, direction]
        if need_out:
            q = qs[rows, cols]
            carried = jnp.dot(q, state.astype(BF16), preferred_element_type=F32)
            if direction == 0:
                s = jnp.dot(q, kt, preferred_element_type=F32)
                o = jnp.dot((s * mask[hh]).astype(BF16), v, preferred_element_type=F32)
                o = o + qdec_f[hh] * carried
                mine, other = accf, accb
            else:
                o = qdec_b[hh] * carried
                mine, other = accb, accf
            if finish:
                o = o + other[rows, cols]
                mu = jnp.mean(o, axis=-1, keepdims=True)
                d = o - mu
                var = jnp.mean(d * d, axis=-1, keepdims=True)
                gate = _silu(g_ref[rows, cols].astype(F32))
                y_ref[rows, cols] = (gate * (d * lax.rsqrt(var + GN_EPS))).astype(BF16)
            else:
                mine[rows, cols] = o
        vdec, cdec = (vdec_f, cdec_f) if direction == 0 else (vdec_b, cdec_b)
        st[hh, direction] = state * cdec[hh] + jnp.dot(
            kt, (v.astype(F32) * vdec[hh]).astype(BF16), preferred_element_type=F32)

    def scan(finish_f, finish_b):
        def body(i, carry):
            for hh in heads:
                visit(hh, 0, i, finish_f)
                visit(hh, 1, n_chunks - 1 - i, finish_b)
            return carry
        return body

    if half:
        lax.fori_loop(0, half, scan(False, False), 0, unroll=min(2, half))
    if n_chunks % 2:
        scan(False, need_out)(half, 0)
    if half:
        lax.fori_loop(n_chunks - half, n_chunks, scan(need_out, need_out), 0, unroll=min(2, half))

    for hh in heads:
        sfo_ref[0, hh] = st[hh, 0]
        sbo_ref[0, hh] = st[hh, 1]


def _retention(z, seg, layer, dec_all, rope, s_f, s_b, need_out, prev):
    bsz, seq, rb0 = seg
    hd = RET_HEAD_DIM
    hps = RET_HPS
    wide = hps * hd
    use_rope = rope is not None
    col = lambda off: pl.BlockSpec((seq, wide), lambda bi, h: (rb0 + bi, off // wide + h))
    state = pl.BlockSpec((1, hps, hd, hd), lambda bi, h: (bi, h, 0, 0))
    in_specs = [pl.BlockSpec((None, hps, 2, hd), lambda bi, h: (layer, h, 0, 0)),
                col(O_Q), col(O_K), col(O_V), col(O_G)]
    args = [dec_all, z, z, z, z]
    if use_rope:
        in_specs += [pl.BlockSpec((seq, LANE), lambda bi, h: (0, 0))] * 2
        args += list(rope)
    in_specs += [state, state]
    args += [s_f, s_b]
    out_specs = [state, state]
    out_shape = [jax.ShapeDtypeStruct((bsz, RET_HEADS, hd, hd), F32)] * 2
    n_chunks = seq // RET_BLOCK
    scratch = [pltpu.VMEM((seq, wide), BF16), pltpu.VMEM((n_chunks, hps, hd, RET_BLOCK), BF16),
               pltpu.VMEM((seq, wide), F32), pltpu.VMEM((seq, wide), F32),
               pltpu.VMEM((hps, 2, hd, hd), F32)]
    kw = dict(grid=(bsz, RET_HEADS // hps), scratch_shapes=scratch,
              compiler_params=_params(("parallel", "parallel"), 56))
    kern = functools.partial(_ret_kernel, use_rope=use_rope, need_out=need_out)
    if not need_out:
        res = pl.pallas_call(kern, in_specs=in_specs, out_specs=out_specs, out_shape=out_shape,
                             name="retention_state", **kw)(*args)
        return None, res[0], res[1]
    out_specs = [pl.BlockSpec((seq, wide), lambda bi, h: (rb0 + bi, h))] + out_specs
    out_shape = [jax.ShapeDtypeStruct((z.shape[0], D_RET), BF16)] + out_shape
    res = _seg_call(kern, prev=prev, n_in=len(args), in_specs=in_specs, out_specs=out_specs,
                    out_shape=out_shape, name="retention", args=args, **kw)
    return res[0], res[1], res[2]


def _rope_tables(seq):
    rows = seq // GRID_W
    row = jnp.repeat(jnp.arange(rows, dtype=F32), GRID_W)
    colp = jnp.tile(jnp.arange(GRID_W, dtype=F32), rows)
    inv = ROPE_BASE ** (-jnp.arange(ROPE_PAIRS, dtype=F32) / ROPE_PAIRS)
    ang_r = row[:, None] * inv[None, :]
    ang_c = colp[:, None] * inv[None, :]
    return (jnp.concatenate([jnp.cos(ang_r), jnp.cos(ang_c)], axis=-1),
            jnp.concatenate([jnp.sin(ang_r), jnp.sin(ang_c)], axis=-1))


def _merge_kernel(ya_ref, yb_ref, yc_ref, ga_ref, gb_ref, gc_ref, pa_ref, pb_ref, pc_ref, o_ref):
    def branch(g_ref, y_ref, p_ref, rows):
        return (jax.nn.sigmoid(g_ref[rows, :].astype(F32))
                * jnp.dot(y_ref[rows, :], p_ref[...], preferred_element_type=F32))

    sub = o_ref.shape[0] // MERGE_SUB
    for s in range(MERGE_SUB):
        rows = slice(s * sub, (s + 1) * sub)
        m = (branch(ga_ref, ya_ref, pa_ref, rows) + branch(gb_ref, yb_ref, pb_ref, rows)
             + branch(gc_ref, yc_ref, pc_ref, rows))
        o_ref[rows, :] = m.astype(BF16)


def _merge(ya, yb, yc, z, p_a, p_b, p_c, m_rows):
    tm, tn = TM_MERGE, TN_WS
    goff = O_GATE // tn
    gstep = D_MODEL // tn
    act = lambda width: pl.BlockSpec((tm, width), lambda i, j: (i, 0))
    gate = lambda br: pl.BlockSpec((tm, tn), lambda i, j: (i, goff + br * gstep + j))
    wgt = lambda rows: pl.BlockSpec((rows, tn), lambda i, j: (0, j))
    return pl.pallas_call(
        _merge_kernel,
        grid=(m_rows // tm, D_MODEL // tn),
        in_specs=[act(D_POOL), act(D_HYENA), act(D_RET), gate(0), gate(1), gate(2),
                  wgt(D_POOL), wgt(D_HYENA), wgt(D_RET)],
        out_specs=pl.BlockSpec((tm, tn), lambda i, j: (i, j)),
        out_shape=jax.ShapeDtypeStruct((m_rows, D_MODEL), BF16),
        compiler_params=_params(("parallel", "arbitrary"), 48),
        name="merge",
    )(ya, yb, yc, z, z, z, p_a, p_b, p_c)


def _res_ln_kernel(*refs, emit_xm, n_k):
    a_ref, w_ref, b_ref, x_ref, gt_ref, g_ref, be_ref = refs[:7]
    refs = refs[7:]
    if emit_xm:
        sh_ref, sc_ref, xo_ref, xm_ref, acc_ref, row_ref = refs
    else:
        xo_ref, acc_ref, row_ref = refs
    k = pl.program_id(1)
    sub = x_ref.shape[0] // RES_SUB

    def spread_rows():
        tile = lambda v: jnp.broadcast_to(v, (LN_ROWS, v.shape[1]))
        gate = gt_ref[0] * (1.0 / DEEPNORM_ALPHA)
        row_ref[0] = tile(gate)
        row_ref[1] = tile(gate * b_ref[0])
        row_ref[2] = tile(g_ref[0])
        row_ref[3] = tile(be_ref[0])
        if emit_xm:
            row_ref[4] = tile(g_ref[0] * (1.0 + sc_ref[0]))
            row_ref[5] = tile(be_ref[0] * (1.0 + sc_ref[0]) + sh_ref[0])

    def finish(r0):
        rr = slice(r0, r0 + LN_ROWS)
        r = x_ref[rr, :] + row_ref[0] * acc_ref[rr, :] + row_ref[1]
        mu = jnp.mean(r, axis=-1, keepdims=True)
        d = r - mu
        var = jnp.mean(d * d, axis=-1, keepdims=True)
        xhat = d * lax.rsqrt(var + LN_EPS / DEEPNORM_ALPHA ** 2)
        xo_ref[rr, :] = xhat * row_ref[2] + row_ref[3]
        if emit_xm:
            xm_ref[rr, :] = (xhat * row_ref[4] + row_ref[5]).astype(BF16)

    def step(first, last):
        if last:
            spread_rows()
        for s in range(RES_SUB):
            rows = slice(s * sub, (s + 1) * sub)
            part = jnp.dot(a_ref[rows, :], w_ref[...], preferred_element_type=F32)
            if first:
                acc_ref[rows, :] = part
            else:
                acc_ref[rows, :] += part
            if last:
                for r0 in range(s * sub, (s + 1) * sub, LN_ROWS):
                    finish(r0)

    if n_k == 1:
        step(True, True)
    else:
        pl.when(k == 0)(lambda: step(True, False))
        pl.when(jnp.logical_and(k > 0, k < n_k - 1))(lambda: step(False, False))
        pl.when(k == n_k - 1)(lambda: step(False, True))


def _res_ln(a, w, b_all, x, mod, layer, gate_blk, ln_g, ln_b, next_mod, m_rows, seq, bsz, name):
    kdim = a.shape[1]
    tm = TM_ROW
    tk = min(TK_ROW, kdim)
    per = seq // tm
    gidx = _mod_index(layer, per, bsz)
    vec = lambda arr: arr.reshape(DEPTH, 1, D_MODEL)
    lvec = pl.BlockSpec((1, 1, D_MODEL), lambda i, k: (layer, 0, 0))
    in_specs = [pl.BlockSpec((tm, tk), lambda i, k: (i, k)),
                pl.BlockSpec((tk, D_MODEL), lambda i, k: (k, 0)),
                lvec,
                pl.BlockSpec((tm, D_MODEL), lambda i, k: (i, 0)),
                pl.BlockSpec((1, 1, D_MODEL), lambda i, k: (gidx(i), 0, gate_blk)),
                lvec, lvec]
    args = [a, w, vec(b_all), x, mod, vec(ln_g), vec(ln_b)]
    row_out = pl.BlockSpec((tm, D_MODEL), lambda i, k: (i, 0))
    out_specs = [row_out]
    out_shape = [jax.ShapeDtypeStruct((m_rows, D_MODEL), F32)]
    if next_mod is not None:
        nl, sh_blk, sc_blk = next_mod
        nidx = _mod_index(nl, per, bsz)
        in_specs += [pl.BlockSpec((1, 1, D_MODEL), lambda i, k: (nidx(i), 0, sh_blk)),
                     pl.BlockSpec((1, 1, D_MODEL), lambda i, k: (nidx(i), 0, sc_blk))]
        args += [mod, mod]
        out_specs.append(row_out)
        out_shape.append(jax.ShapeDtypeStruct((m_rows, D_MODEL), BF16))
    res = pl.pallas_call(
        functools.partial(_res_ln_kernel, emit_xm=next_mod is not None, n_k=kdim // tk),
        grid=(m_rows // tm, kdim // tk),
        in_specs=in_specs,
        out_specs=out_specs,
        out_shape=out_shape,
        scratch_shapes=[pltpu.VMEM((tm, D_MODEL), F32), pltpu.VMEM((6, LN_ROWS, D_MODEL), F32)],
        compiler_params=_params(("parallel", "arbitrary"), 56),
        name=name,
    )(*args)
    return (res[0], res[1]) if next_mod is not None else (res[0], None)


def kernel(x, c, ctx, c_ctx, w_ada, b_ada, w_in, b_in, conv_w, conv_b, pool_w, pool_scale, filt_w1, filt_b1, filt_f1, filt_w2, filt_b2, filt_f2, filt_w3, filt_b3, filt_f3, filt_w4, hyena_d, ret_decay, p_a, p_b, p_c, w_o, b_o, ln1_g, ln1_b, w_mlp1, b_mlp1, w_mlp2, b_mlp2, ln2_g, ln2_b):
    bsz, seq, _ = x.shape
    ctx_len = ctx.shape[1]
    assert x.shape == (bsz, seq, D_MODEL) and ctx.shape == (bsz, ctx_len, D_MODEL)
    assert seq % RET_BLOCK == 0 and ctx_len % RET_BLOCK == 0 and seq % GRID_W == 0
    assert bsz + 1 <= ADA_ROWS and seq % ctx_len == 0
    n_lat, n_ctx = bsz * seq, bsz * ctx_len
    m_total = n_lat + n_ctx
    assert seq % TM_MERGE == 0 and n_ctx % TM_MERGE == 0 and seq % TM_ROW == 0 and n_ctx % TM_ROW == 0
    assert O_Q % TN_WS == 0 and O_V % TN_WS == 0
    seg_x = (bsz, seq, 0)
    seg_c = (bsz, ctx_len, n_lat // ctx_len)

    cvec = jnp.concatenate([c, c_ctx[None, :], jnp.zeros((ADA_ROWS - bsz - 1, D_MODEL), F32)], axis=0)
    mod = _ada(cvec, w_ada, b_ada).reshape(DEPTH * ADA_ROWS, 1, 6 * D_MODEL)

    deltas = _filter_deltas()
    dft_x, dft_c = _dft_matrices(seq), _dft_matrices(ctx_len)
    zfeat_x, zfeat_c = _filter_features(seq), _filter_features(ctx_len)
    rope = _rope_tables(seq)
    zero_state = jnp.zeros((bsz, RET_HEADS, RET_HEAD_DIM, RET_HEAD_DIM), F32)

    row3 = lambda a: a.reshape(DEPTH, 1, -1)
    fp = {'w1': jnp.pad(filt_w1, ((0, 0), (0, FEAT_PAD - FILTER_EMB), (0, 0))),
          'b1': row3(filt_b1), 'f1': row3(filt_f1), 'w2': filt_w2, 'b2': row3(filt_b2), 'f2': row3(filt_f2),
          'w3': filt_w3, 'b3': row3(filt_b3), 'f3': row3(filt_f3), 'w4': filt_w4}
    dec_all = jnp.broadcast_to(jnp.swapaxes(ret_decay, 1, 2)[:, :, :, None],
                               (DEPTH, RET_HEADS, 2, RET_HEAD_DIM))
    kspec_x = _filter_spectrum(dft_x[0], _filters(zfeat_x, deltas, fp))
    kspec_c = _filter_spectrum(dft_c[0], _filters(zfeat_c, deltas, fp))
    xs, xm = _mod0(x.reshape(n_lat, D_MODEL), ctx.reshape(n_ctx, D_MODEL), mod, seq, bsz)

    def hyena(z, seg, l, dft, kspec, prev):
        fwd, inv = dft
        uu, x0 = _hy_pre(z, seg, l, conv_w, conv_b)
        return _hy_inv(fwd, inv, _hy_fwd(fwd, uu, kspec, l, hyena_d), x0, seg, m_total, prev)

    ya = jnp.zeros((m_total, D_POOL), BF16)
    yb = jnp.zeros((m_total, D_HYENA), BF16)
    yc = jnp.zeros((m_total, D_RET), BF16)
    qk_tiles = tuple(range(O_Q // TN_WS, O_V // TN_WS))
    for l in range(DEPTH):
        last = l == DEPTH - 1
        rows = n_lat if last else m_total
        z, wo_b, pa_b, pb_b, pc_b = _wsmm(xm, w_in, b_in, l, m_total, False, "in_proj",
                                          sides=(w_o, p_a, p_b, p_c), regroup_tiles=qk_tiles)
        if not last:
            ya = _pool(z, seg_c, l, pool_w, pool_scale, ya)
            yb = hyena(z, seg_c, l, dft_c, kspec_c, yb)
            yc, s_f, s_b = _retention(z, seg_c, l, dec_all, None, zero_state, zero_state, True, yc)
        else:
            _, s_f, s_b = _retention(z, seg_c, l, dec_all, None, zero_state, zero_state, False, None)
        ya = _pool(z, seg_x, l, pool_w, pool_scale, ya)
        yb = hyena(z, seg_x, l, dft_x, kspec_x, yb)
        yc, _, _ = _retention(z, seg_x, l, dec_all, rope, s_f, s_b, True, yc)
        merged = _merge(ya, yb, yc, z, pa_b, pb_b, pc_b, rows)
        xs, xm = _res_ln(merged, wo_b, b_o, xs, mod, l, 2, ln1_g, ln1_b, (l, 3, 4), rows, seq, bsz,
                         "out_proj_ln1")
        hid, w2_b = _wsmm(xm, w_mlp1, b_mlp1, l, rows, True, "mlp_up", sides=(w_mlp2,))
        xs, xm = _res_ln(hid, w2_b, b_mlp2, xs, mod, l, 5, ln2_g, ln2_b,
                         None if last else (l + 1, 0, 1), rows, seq, bsz, "mlp_down_ln2")
    return xs.reshape(bsz, seq, D_MODEL)
```

```python
import functools
import math

import jax
import jax.numpy as jnp
from jax import lax
from jax.experimental import pallas as pl
from jax.experimental.pallas import tpu as pltpu

F32 = jnp.float32
BF16 = jnp.bfloat16

D_MODEL = 2048
DEPTH = 4
GRID_W = 64
D_POOL = D_MODEL // 4
POOL_WINDOWS = (2, 4, 8, 16)
POOL_GROUP = D_POOL // len(POOL_WINDOWS)
D_HYENA = D_MODEL // 4
FILTER_EMB = 33
FILTER_BANDS = (FILTER_EMB - 1) // 2
FILTER_ORDER = 64
FILTER_DECAY_TARGET = 1e-2
FILTER_FAST_PCT = 0.3
FILTER_SLOW_PCT = 1.5
RET_HEAD_DIM = 256
D_RET = D_MODEL // 2
RET_HEADS = D_RET // RET_HEAD_DIM
RET_CHUNK = 128
ROPE_BASE = 10000.0
ROPE_PAIRS = RET_HEAD_DIM // 4
N_BRANCH = 3
D_FF = 4 * D_MODEL
LN_EPS = 1e-5
GN_EPS = 1e-6
DEEPNORM_ALPHA = (2 * DEPTH) ** 0.25
O_POOL = 0
O_HY = O_POOL + D_POOL
O_Q = O_HY + 3 * D_HYENA
O_K = O_Q + D_RET
O_V = O_K + D_RET
O_G = O_V + D_RET
O_GATE = O_G + D_RET
D_IN = O_GATE + N_BRANCH * D_MODEL

LANE = 128
FEAT_PAD = LANE
POOL_PAD = 16
ADA_ROWS = 8
MIB = 1024 * 1024

TM_WS = (1536, 1024)
TM_MERGE = 1024
TN_WS = 1024
WS_SLABS = 4
TM_ROW = 512
TK_ROW = 2048
RES_SUB = 4
MERGE_SUB = 2
LN_ROWS = 16
HY_CT = 256
HY_FT = 512
HY_TM = 1024
HY_SUB = 2
DFT_RADIX = 64
RET_HPS = 2
RET_BLOCK = 256


def _params(semantics, vmem_mib):
    return pltpu.CompilerParams(dimension_semantics=semantics, vmem_limit_bytes=vmem_mib * MIB)


def _silu(v):
    return v * jax.nn.sigmoid(v)


def _layer_norm(r, g, b):
    mu = jnp.mean(r, axis=-1, keepdims=True)
    d = r - mu
    var = jnp.mean(d * d, axis=-1, keepdims=True)
    return d * lax.rsqrt(var + LN_EPS) * g + b


def _skip_ref(kern, idx):
    def wrapped(*refs):
        return kern(*refs[:idx], *refs[idx + 1:])
    return wrapped


def _seg_call(kern, *, prev, n_in, **kw):
    in_specs = list(kw.pop("in_specs")) + [pl.BlockSpec(memory_space=pl.ANY)]
    args = list(kw.pop("args")) + [prev]
    return pl.pallas_call(_skip_ref(kern, n_in), in_specs=in_specs,
                          input_output_aliases={n_in: 0}, **kw)(*args)


def _mod_index(layer, per_batch_tiles, bsz):
    return lambda i: layer * ADA_ROWS + jnp.minimum(i // per_batch_tiles, bsz)


def _ada_kernel(c_ref, w_ref, b_ref, o_ref):
    s = _silu(c_ref[...]).astype(BF16)
    o_ref[0] = jnp.dot(s, w_ref[0].astype(BF16), preferred_element_type=F32) + b_ref[0]


def _ada(cvec, w_ada, b_ada):
    tn = 1024
    n = w_ada.shape[2]
    return pl.pallas_call(
        _ada_kernel,
        grid=(DEPTH, n // tn),
        in_specs=[
            pl.BlockSpec((ADA_ROWS, D_MODEL), lambda l, j: (0, 0)),
            pl.BlockSpec((1, D_MODEL, tn), lambda l, j: (l, 0, j)),
            pl.BlockSpec((1, 1, tn), lambda l, j: (l, 0, j)),
        ],
        out_specs=pl.BlockSpec((1, ADA_ROWS, tn), lambda l, j: (l, 0, j)),
        out_shape=jax.ShapeDtypeStruct((DEPTH, ADA_ROWS, n), F32),
        compiler_params=_params(("parallel", "parallel"), 40),
        name="ada",
    )(cvec, w_ada, b_ada.reshape(DEPTH, 1, n))


def _mod0_kernel(x_ref, c_ref, sh_ref, sc_ref, xo_ref, xm_ref, *, n_lat_tiles):
    def emit(v):
        xo_ref[...] = v
        xm_ref[...] = (v * (1.0 + sc_ref[0]) + sh_ref[0]).astype(BF16)

    i = pl.program_id(0)

    @pl.when(i < n_lat_tiles)
    def _():
        emit(x_ref[...])

    @pl.when(i >= n_lat_tiles)
    def _():
        emit(c_ref[...])


def _mod0(x2d, c2d, mod, seq, bsz):
    n_lat, n_ctx = x2d.shape[0], c2d.shape[0]
    tm = TM_ROW
    nl = n_lat // tm
    midx = _mod_index(0, seq // tm, bsz)
    return pl.pallas_call(
        functools.partial(_mod0_kernel, n_lat_tiles=nl),
        grid=((n_lat + n_ctx) // tm,),
        in_specs=[pl.BlockSpec((tm, D_MODEL), lambda i: (jnp.minimum(i, nl - 1), 0)),
                  pl.BlockSpec((tm, D_MODEL), lambda i: (jnp.maximum(i - nl, 0), 0)),
                  pl.BlockSpec((1, 1, D_MODEL), lambda i: (midx(i), 0, 0)),
                  pl.BlockSpec((1, 1, D_MODEL), lambda i: (midx(i), 0, 1))],
        out_specs=[pl.BlockSpec((tm, D_MODEL), lambda i: (i, 0)),
                   pl.BlockSpec((tm, D_MODEL), lambda i: (i, 0))],
        out_shape=[jax.ShapeDtypeStruct((n_lat + n_ctx, D_MODEL), F32),
                   jax.ShapeDtypeStruct((n_lat + n_ctx, D_MODEL), BF16)],
        compiler_params=_params(("parallel",), 40),
        name="assemble_modulate",
    )(x2d, c2d, mod, mod)


def _regroup(a, b):
    lo = lax.broadcasted_iota(jnp.int32, a.shape, 1) < LANE // 2
    return (jnp.where(lo, a, pltpu.roll(b, LANE // 2, 1)),
            jnp.where(lo, pltpu.roll(a, LANE // 2, 1), b))


def _wsmm_kernel(*refs, sq_relu, side_outer, regroup_tiles):
    n_side = len(side_outer)
    x_ref, w_ref, b_ref = refs[:3]
    s_refs = refs[3:3 + n_side]
    o_ref = refs[3 + n_side]
    so_refs = refs[4 + n_side:4 + 2 * n_side]
    wb_ref, bb_ref = refs[4 + 2 * n_side:]
    j = pl.program_id(0)
    i = pl.program_id(1)
    first = i == 0
    tn = wb_ref.shape[1]
    for s_ref, so_ref, n_outer in zip(s_refs, so_refs, side_outer):
        @pl.when(jnp.logical_and(j < n_outer, i < WS_SLABS))
        def _(s_ref=s_ref, so_ref=so_ref):
            so_ref[...] = s_ref[0].astype(BF16)

    def plain():
        wb_ref[...] = w_ref[0].astype(BF16)
        bb_ref[...] = jnp.broadcast_to(b_ref[0], bb_ref.shape)

    def regrouped():
        b8 = jnp.broadcast_to(b_ref[0], bb_ref.shape)
        for c0 in range(0, tn, 2 * LANE):
            lo, hi = slice(c0, c0 + LANE), slice(c0 + LANE, c0 + 2 * LANE)
            wa, wc = _regroup(w_ref[0, :, lo], w_ref[0, :, hi])
            wb_ref[:, lo] = wa.astype(BF16)
            wb_ref[:, hi] = wc.astype(BF16)
            ba, bc = _regroup(b8[:, lo], b8[:, hi])
            bb_ref[:, lo] = ba
            bb_ref[:, hi] = bc

    if regroup_tiles:
        hit = functools.reduce(jnp.logical_or, [j == t for t in regroup_tiles])
        pl.when(jnp.logical_and(first, hit))(regrouped)
        pl.when(jnp.logical_and(first, jnp.logical_not(hit)))(plain)
    else:
        pl.when(first)(plain)

    y = jnp.dot(x_ref[...], wb_ref[...], preferred_element_type=F32) + bb_ref[0:1, :]
    if sq_relu:
        y = jnp.square(jnp.maximum(y, 0.0))
    o_ref[...] = y.astype(BF16)


def _wsmm(xm, w_all, b_all, layer, m_rows, sq_relu, name, sides=(), regroup_tiles=()):
    kdim, n = w_all.shape[1], w_all.shape[2]
    tm = next(t for t in TM_WS if m_rows % t == 0)
    tn = TN_WS
    n_j, n_i = n // tn, m_rows // tm
    assert n_i >= WS_SLABS
    side_outer = []
    in_specs = [pl.BlockSpec((tm, kdim), lambda j, i: (i, 0)),
                pl.BlockSpec((1, kdim, tn), lambda j, i: (layer, 0, j)),
                pl.BlockSpec((1, 1, tn), lambda j, i: (layer, 0, j))]
    args = [xm, w_all, b_all.reshape(DEPTH, 1, n)]
    out_specs = [pl.BlockSpec((tm, tn), lambda j, i: (i, j))]
    out_shape = [jax.ShapeDtypeStruct((xm.shape[0], n), BF16)]
    for side in sides:
        r, c = side.shape[1:]
        rows = 16
        while r % (rows * WS_SLABS) or r // (rows * WS_SLABS) > n_j:
            rows *= 2
            assert rows * WS_SLABS <= r
        n_outer = r // (rows * WS_SLABS)
        side_outer.append(n_outer)
        sidx = functools.partial(
            lambda j, i, n_outer: jnp.where(j < n_outer, j * WS_SLABS + jnp.minimum(i, WS_SLABS - 1),
                                            n_outer * WS_SLABS - 1),
            n_outer=n_outer)
        in_specs.append(pl.BlockSpec((1, rows, c), functools.partial(
            lambda j, i, sidx: (layer, sidx(j, i), 0), sidx=sidx)))
        args.append(side)
        out_specs.append(pl.BlockSpec((rows, c), functools.partial(
            lambda j, i, sidx: (sidx(j, i), 0), sidx=sidx)))
        out_shape.append(jax.ShapeDtypeStruct((r, c), BF16))
    return pl.pallas_call(
        functools.partial(_wsmm_kernel, sq_relu=sq_relu, side_outer=tuple(side_outer),
                          regroup_tiles=regroup_tiles),
        grid=(n_j, n_i),
        in_specs=in_specs,
        out_specs=out_specs,
        out_shape=out_shape,
        scratch_shapes=[pltpu.VMEM((kdim, tn), BF16), pltpu.VMEM((8, tn), F32)],
        compiler_params=_params(("parallel", "arbitrary"), 54),
        name=name,
    )(*args)


def _pool_kernel(z_ref, w_ref, s_ref, o_ref, pad_ref):
    seq = z_ref.shape[0]
    rows = seq + 2 * POOL_PAD
    zeros = jnp.zeros((POOL_PAD, POOL_GROUP), F32)
    pad_ref[0:POOL_PAD, :] = zeros
    pad_ref[POOL_PAD + seq:rows, :] = zeros
    t = lax.broadcasted_iota(jnp.int32, (seq, POOL_GROUP), 0)
    for g, win in enumerate(POOL_WINDOWS):
        cols = slice(g * POOL_GROUP, (g + 1) * POOL_GROUP)
        u = z_ref[:, cols].astype(F32)
        pad_ref[POOL_PAD:POOL_PAD + seq, :] = u
        w = pad_ref[...]
        w = pltpu.roll(w, 1, 0) + w
        width = 2
        while width < win:
            half = width // 2
            w = pltpu.roll(w, half, 0) + pltpu.roll(w, rows - half, 0)
            width *= 2
        half = win // 2
        count = jnp.minimum(t + half, seq) - jnp.maximum(t - half, 0)
        pooled = w[POOL_PAD:POOL_PAD + seq, :] / count.astype(F32) - u
        y = jnp.dot(pooled.astype(BF16), w_ref[0, g].astype(BF16), preferred_element_type=F32)
        o_ref[:, cols] = (y * s_ref[0, :, cols]).astype(BF16)


def _pool(z, seg, layer, pool_w, pool_scale, prev):
    bsz, seq, rb0 = seg
    ng = len(POOL_WINDOWS)
    return _seg_call(
        _pool_kernel, prev=prev, n_in=3,
        grid=(bsz,),
        in_specs=[pl.BlockSpec((seq, D_POOL), lambda bi: (rb0 + bi, O_POOL // D_POOL)),
                  pl.BlockSpec((1, ng, POOL_GROUP, POOL_GROUP), lambda bi: (layer, 0, 0, 0)),
                  pl.BlockSpec((1, 1, D_POOL), lambda bi: (layer, 0, 0))],
        out_specs=pl.BlockSpec((seq, D_POOL), lambda bi: (rb0 + bi, 0)),
        out_shape=jax.ShapeDtypeStruct((z.shape[0], D_POOL), BF16),
        scratch_shapes=[pltpu.VMEM((seq + 2 * POOL_PAD, POOL_GROUP), F32)],
        compiler_params=_params(("parallel",), 48),
        name="pool",
        args=[z, pool_w, pool_scale.reshape(DEPTH, 1, D_POOL)],
    )


def _conv3(u, w, b):
    seq = u.shape[0]
    t = lax.broadcasted_iota(jnp.int32, u.shape, 0)
    prev = jnp.where(t == 0, 0.0, pltpu.roll(u, 1, 0))
    nxt = jnp.where(t == seq - 1, 0.0, pltpu.roll(u, seq - 1, 0))
    return prev * w[0:1, :] + u * w[1:2, :] + nxt * w[2:3, :] + b


def _hy_pre_kernel(zv_ref, z0_ref, z1_ref, wv_ref, w0_ref, w1_ref, bv_ref, b0_ref, b1_ref,
                   uu_ref, x0_ref):
    v = _conv3(zv_ref[...].astype(F32), wv_ref[0], bv_ref[0])
    x1 = _conv3(z1_ref[...].astype(F32), w1_ref[0], b1_ref[0])
    uu_ref[0] = (v * x1).astype(BF16)
    x0_ref[0] = _conv3(z0_ref[...].astype(F32), w0_ref[0], b0_ref[0]).astype(BF16)


def _hy_pre(z, seg, layer, conv_w, conv_b):
    bsz, seq, rb0 = seg
    ct = HY_CT
    nct = D_HYENA // ct
    zoff = O_HY // ct
    zspec = [pl.BlockSpec((seq, ct), functools.partial(lambda bi, j, s: (rb0 + bi, zoff + s * nct + j), s=s))
             for s in range(3)]
    wspec = [pl.BlockSpec((1, 3, ct), functools.partial(lambda bi, j, s: (layer, 0, s * nct + j), s=s))
             for s in range(3)]
    bspec = [pl.BlockSpec((1, 1, ct), functools.partial(lambda bi, j, s: (layer, 0, s * nct + j), s=s))
             for s in range(3)]
    cb = conv_b.reshape(DEPTH, 1, 3 * D_HYENA)
    out = pl.BlockSpec((1, seq, ct), lambda bi, j: (bi, 0, j))
    return pl.pallas_call(
        _hy_pre_kernel,
        grid=(bsz, nct),
        in_specs=zspec + wspec + bspec,
        out_specs=[out, out],
        out_shape=[jax.ShapeDtypeStruct((bsz, seq, D_HYENA), BF16)] * 2,
        compiler_params=_params(("parallel", "parallel"), 48),
        name="hy_pre",
    )(z, z, z, conv_w, conv_w, conv_w, cb, cb, cb)


def _filt_kernel(zf_ref, w1_ref, b1_ref, f1_ref, w2_ref, b2_ref, f2_ref, w3_ref, b3_ref, f3_ref,
                 w4_ref, dl_ref, o_ref):
    tl = zf_ref.shape[0]
    zf = zf_ref[...]

    def dense(a, w_ref):
        return jnp.dot(a.astype(BF16), w_ref[0].astype(BF16), preferred_element_type=F32)

    hdn = jnp.sin(f1_ref[0] * (dense(zf, w1_ref) + b1_ref[0]))
    hdn = jnp.sin(f2_ref[0] * (dense(hdn, w2_ref) + b2_ref[0]))
    hdn = jnp.sin(f3_ref[0] * (dense(hdn, w3_ref) + b3_ref[0]))
    h = dense(hdn, w4_ref)
    decay = jnp.exp(-zf[:, 0:1] * jnp.abs(dl_ref[...]))
    row = lax.broadcasted_iota(jnp.int32, (tl, D_HYENA), 0) + pl.program_id(1) * tl
    o_ref[:, 0:D_HYENA] = (h[:, 0:D_HYENA] * decay).astype(BF16)
    o_ref[:, D_HYENA:] = jnp.where(row == 0, 0.0, h[:, D_HYENA:] * decay).astype(BF16)


def _filter_features(seq):
    t = jnp.linspace(0.0, 1.0, seq, dtype=F32)[:, None]
    w = 2.0 * math.pi * jnp.arange(seq, dtype=F32)[:, None] / seq
    f = jnp.linspace(1e-4, FILTER_BANDS - 1, FILTER_BANDS, dtype=F32)[None, :]
    z = jnp.concatenate([t, jnp.cos(f * w), -jnp.sin(f * w)], axis=-1)
    return jnp.pad(z, ((0, 0), (0, FEAT_PAD - FILTER_EMB)))


def _filter_deltas():
    max_decay = math.log(FILTER_DECAY_TARGET) / FILTER_FAST_PCT
    min_decay = math.log(FILTER_DECAY_TARGET) / FILTER_SLOW_PCT
    return jnp.linspace(min_decay, max_decay, D_HYENA, dtype=F32)[None, :]


def _filters(zfeat, deltas, fp):
    seq = zfeat.shape[0]
    tl = min(256, seq)
    lsel = lambda shape: pl.BlockSpec((1,) + shape, lambda l, i: (l,) + (0,) * len(shape))
    vec = lsel((1, FILTER_ORDER))
    sq = lsel((FILTER_ORDER, FILTER_ORDER))
    return pl.pallas_call(
        _filt_kernel,
        grid=(DEPTH, seq // tl),
        in_specs=[pl.BlockSpec((tl, FEAT_PAD), lambda l, i: (i, 0)),
                  lsel((FEAT_PAD, FILTER_ORDER)), vec, vec, sq, vec, vec, sq, vec, vec,
                  lsel((FILTER_ORDER, 2 * D_HYENA)),
                  pl.BlockSpec((1, D_HYENA), lambda l, i: (0, 0))],
        out_specs=pl.BlockSpec((tl, 2 * D_HYENA), lambda l, i: (i, l)),
        out_shape=jax.ShapeDtypeStruct((seq, DEPTH * 2 * D_HYENA), BF16),
        compiler_params=_params(("parallel", "parallel"), 32),
        name="hy_filter",
    )(zfeat, fp['w1'], fp['b1'], fp['f1'], fp['w2'], fp['b2'], fp['f2'], fp['w3'], fp['b3'], fp['f3'],
      fp['w4'], deltas)


def _dft_kernel(ca_ref, sa_ref, cb_ref, sb_ref, fwd_ref, inv_ref):
    ca, sa = ca_ref[0], sa_ref[0]
    cb, sb = cb_ref[...], sb_ref[...]
    cosb = ca * cb - sa * sb
    sinb = sa * cb + ca * sb
    row = lax.broadcasted_iota(jnp.int32, cosb.shape, 0) + pl.program_id(0) * cosb.shape[0]
    col = lax.broadcasted_iota(jnp.int32, cosb.shape, 1)
    alt = lambda idx: jnp.where(jnp.bitwise_and(idx, 1) == 0, 1.0, -1.0)
    fwd_ref[0] = cosb.astype(BF16)
    fwd_ref[1] = jnp.where(row == 0, alt(col), sinb).astype(BF16)
    inv_ref[...] = jnp.where(col == 0, alt(row), sinb).astype(BF16)


def _dft_matrices(seq):
    n = 2 * seq
    radix = DFT_RADIX
    t = jnp.arange(seq, dtype=jnp.int32)[None, :]
    ang = lambda f: ((f * t) % n).astype(F32) * (2.0 * math.pi / n)
    ang_a = ang(radix * jnp.arange(seq // radix, dtype=jnp.int32)[:, None])[:, None, :]
    ang_b = ang(jnp.arange(radix, dtype=jnp.int32)[:, None])
    coarse = pl.BlockSpec((1, 1, seq), lambda i: (i, 0, 0))
    fine = pl.BlockSpec((radix, seq), lambda i: (0, 0))
    return pl.pallas_call(
        _dft_kernel,
        grid=(seq // radix,),
        in_specs=[coarse, coarse, fine, fine],
        out_specs=[pl.BlockSpec((2, radix, seq), lambda i: (0, i, 0)),
                   pl.BlockSpec((radix, seq), lambda i: (i, 0))],
        out_shape=[jax.ShapeDtypeStruct((2, seq, seq), BF16), jax.ShapeDtypeStruct((seq, seq), BF16)],
        compiler_params=_params(("parallel",), 32),
        name="dft_tables",
    )(jnp.cos(ang_a), jnp.sin(ang_a), jnp.cos(ang_b), jnp.sin(ang_b))


def _mm_kernel(a_ref, b_ref, o_ref):
    o_ref[0] = jnp.dot(a_ref[0], b_ref[...], preferred_element_type=F32)


def _filter_spectrum(fwd, hcat):
    _, seq, _ = fwd.shape
    n = hcat.shape[1]
    tm = min(HY_TM, seq)
    tn = 1024
    return pl.pallas_call(
        _mm_kernel,
        grid=(2, seq // tm, n // tn),
        in_specs=[pl.BlockSpec((1, tm, seq), lambda h, i, j: (h, i, 0)),
                  pl.BlockSpec((seq, tn), lambda h, i, j: (0, j))],
        out_specs=pl.BlockSpec((1, tm, tn), lambda h, i, j: (h, i, j)),
        out_shape=jax.ShapeDtypeStruct((2, seq, n), F32),
        compiler_params=_params(("parallel", "parallel", "parallel"), 40),
        name="hy_filter_dft",
    )(fwd, hcat)


def _hy_fwd_kernel(f_ref, uu_ref, kf_ref, kb_ref, d_ref, y_ref, *, n_fft):
    tm = f_ref.shape[1]
    uu = uu_ref[0]
    sub = tm // HY_SUB
    for s in range(HY_SUB):
        rows = slice(s * sub, (s + 1) * sub)
        a = jnp.dot(f_ref[0, rows, :], uu, preferred_element_type=F32)
        b = jnp.dot(f_ref[1, rows, :], uu, preferred_element_type=F32)
        row0 = (lax.broadcasted_iota(jnp.int32, a.shape, 0) + (pl.program_id(0) * tm + s * sub)) == 0
        ka = kf_ref[0, rows, :] + kb_ref[0, rows, :] + d_ref[0]
        kb_sum = kf_ref[1, rows, :] + kb_ref[1, rows, :] + d_ref[0]
        kb_dif = kf_ref[1, rows, :] - kb_ref[1, rows, :]
        ya = jnp.where(row0, a * ka, a * ka - b * kb_dif)
        yb = jnp.where(row0, b * kb_sum, a * kb_dif + b * ka)
        wgt = jnp.where(row0, 1.0 / n_fft, 2.0 / n_fft)
        y_ref[0, 0, rows, :] = (ya * wgt).astype(BF16)
        y_ref[0, 1, rows, :] = (yb * wgt).astype(BF16)


def _hy_fwd(fwd, uu, kspec, layer, hyena_d):
    bsz, seq, _ = uu.shape
    ct = HY_FT
    nct = D_HYENA // ct
    tm = min(HY_TM, seq)
    return pl.pallas_call(
        functools.partial(_hy_fwd_kernel, n_fft=2 * seq),
        grid=(seq // tm, nct, bsz),
        in_specs=[pl.BlockSpec((2, tm, seq), lambda i, j, bi: (0, i, 0)),
                  pl.BlockSpec((1, seq, ct), lambda i, j, bi: (bi, 0, j)),
                  pl.BlockSpec((2, tm, ct), lambda i, j, bi: (0, i, 2 * nct * layer + j)),
                  pl.BlockSpec((2, tm, ct), lambda i, j, bi: (0, i, 2 * nct * layer + nct + j)),
                  pl.BlockSpec((1, 1, ct), lambda i, j, bi: (layer, 0, j))],
        out_specs=pl.BlockSpec((1, 2, tm, ct), lambda i, j, bi: (bi, 0, i, j)),
        out_shape=jax.ShapeDtypeStruct((bsz, 2, seq, D_HYENA), BF16),
        compiler_params=_params(("parallel", "parallel", "parallel"), 56),
        name="hy_fwd_dft",
    )(fwd, uu, kspec, kspec, hyena_d.reshape(DEPTH, 1, D_HYENA))


def _hy_inv_kernel(fc_ref, fs_ref, y_ref, x0_ref, o_ref):
    y = (jnp.dot(fc_ref[0], y_ref[0, 0], preferred_element_type=F32)
         + jnp.dot(fs_ref[...], y_ref[0, 1], preferred_element_type=F32))
    o_ref[...] = (y * x0_ref[0].astype(F32)).astype(BF16)


def _hy_inv(fwd, inv, yspec, x0, seg, m_total, prev):
    bsz, seq, rb0 = seg
    ct = HY_FT
    nct = D_HYENA // ct
    tm = min(HY_TM, seq)
    per = seq // tm
    return _seg_call(
        _hy_inv_kernel, prev=prev, n_in=4,
        grid=(per, nct, bsz),
        in_specs=[pl.BlockSpec((1, tm, seq), lambda i, j, bi: (0, i, 0)),
                  pl.BlockSpec((tm, seq), lambda i, j, bi: (i, 0)),
                  pl.BlockSpec((1, 2, seq, ct), lambda i, j, bi: (bi, 0, 0, j)),
                  pl.BlockSpec((1, tm, ct), lambda i, j, bi: (bi, i, j))],
        out_specs=pl.BlockSpec((tm, ct), lambda i, j, bi: ((rb0 + bi) * per + i, j)),
        out_shape=jax.ShapeDtypeStruct((m_total, D_HYENA), BF16),
        compiler_params=_params(("parallel", "parallel", "parallel"), 56),
        name="hy_inv_dft",
        args=[fwd, inv, yspec, x0],
    )


def _ret_kernel(*refs, use_rope, need_out):
    refs = list(refs)
    dec_ref, q_ref, k_ref, v_ref, g_ref = refs[:5]
    refs = refs[5:]
    if use_rope:
        cos_ref, sin_ref = refs[:2]
        refs = refs[2:]
    sf0_ref, sb0_ref = refs[:2]
    refs = refs[2:]
    if need_out:
        y_ref = refs[0]
        refs = refs[1:]
    sfo_ref, sbo_ref, qs, kst, accf, accb, st = refs

    seq = k_ref.shape[0]
    csz = RET_BLOCK
    n_chunks = seq // csz
    half = n_chunks // 2
    hd = RET_HEAD_DIM
    heads = range(RET_HPS)

    pos = lax.broadcasted_iota(jnp.int32, (csz, hd), 0).astype(F32)
    ii = lax.broadcasted_iota(jnp.int32, (csz, csz), 0)
    jj = lax.broadcasted_iota(jnp.int32, (csz, csz), 1)
    rel = (ii - jj).astype(F32)
    qdec_f, vdec_f, qdec_b, vdec_b, cdec_f, cdec_b, mask = [], [], [], [], [], [], []
    for hh in heads:
        lg = jnp.log1p(-jnp.exp(dec_ref[hh]))
        lgf = lg[0:1, :]
        lgb = lg[1:2, :]
        qdec_f.append(jnp.exp((pos + 1.0) * lgf))
        vdec_f.append(jnp.exp((csz - 1.0 - pos) * lgf))
        qdec_b.append(jnp.exp((csz - pos) * lgb))
        vdec_b.append(jnp.exp(pos * lgb))
        cdec_f.append(jnp.exp(csz * lgf))
        cdec_b.append(jnp.exp(csz * lgb))
        mask.append(jnp.where(rel >= 0, jnp.exp(jnp.maximum(rel, 0.0) * lgf[:, :csz]), 0.0)
                    + jnp.where(rel <= 0, jnp.exp(jnp.maximum(-rel, 0.0) * lgb[:, :csz]), 0.0))

    def chunk_rows(c):
        return pl.ds(pl.multiple_of(c * csz, csz), csz)

    def rope(x, rows):
        if not use_rope:
            return x
        a, b = x[:, :LANE], x[:, LANE:]
        cs, sn = cos_ref[rows, :], sin_ref[rows, :]
        return jnp.concatenate([a * cs - b * sn, b * cs + a * sn], axis=1)

    def prep(c, carry):
        rows = chunk_rows(c)
        for hh in heads:
            cols = slice(hh * hd, (hh + 1) * hd)
            kc = rope(k_ref[rows, cols].astype(F32) * (hd ** -0.5), rows)
            kst[c, hh] = kc.T.astype(BF16)
            if need_out:
                qs[rows, cols] = rope(q_ref[rows, cols].astype(F32), rows).astype(BF16)
        return carry

    lax.fori_loop(0, n_chunks, prep, 0, unroll=min(2, n_chunks))

    for hh in heads:
        st[hh, 0] = sf0_ref[0, hh]
        st[hh, 1] = sb0_ref[0, hh]

    def visit(hh, direction, c, finish):
        cols = slice(hh * hd, (hh + 1) * hd)
        rows = chunk_rows(c)
        kt = kst[c, hh]
        v = v_ref[rows, cols]
        state = st[hh, direction]
        if need_out:
            q = qs[rows, cols]
            carried = jnp.dot(q, state.astype(BF16), preferred_element_type=F32)
            if direction == 0:
                s = jnp.dot(q, kt, preferred_element_type=F32)
                o = jnp.dot((s * mask[hh]).astype(BF16), v, preferred_element_type=F32)
                o = o + qdec_f[hh] * carried
                mine, other = accf, accb
            else:
                o = qdec_b[hh] * carried
                mine, other = accb, accf
            if finish:
                o = o + other[rows, cols]
                mu = jnp.mean(o, axis=-1, keepdims=True)
                d = o - mu
                var = jnp.mean(d * d, axis=-1, keepdims=True)
                gate = _silu(g_ref[rows, cols].astype(F32))
                y_ref[rows, cols] = (gate * (d * lax.rsqrt(var + GN_EPS))).astype(BF16)
            else:
                mine[rows, cols] = o
        vdec, cdec = (vdec_f, cdec_f) if direction == 0 else (vdec_b, cdec_b)
        st[hh, direction] = state * cdec[hh] + jnp.dot(
            kt, (v.astype(F32) * vdec[hh]).astype(BF16), preferred_element_type=F32)

    def scan(finish_f, finish_b):
        def body(i, carry):
            for hh in heads:
                visit(hh, 0, i, finish_f)
                visit(hh, 1, n_chunks - 1 - i, finish_b)
            return carry
        return body

    if half:
        lax.fori_loop(0, half, scan(False, False), 0, unroll=min(2, half))
    if n_chunks % 2:
        scan(False, need_out)(half, 0)
    if half:
        lax.fori_loop(n_chunks - half, n_chunks, scan(need_out, need_out), 0, unroll=min(2, half))

    for hh in heads:
        sfo_ref[0, hh] = st[hh, 0]
        sbo_ref[0, hh] = st[hh, 1]


def _retention(z, seg, layer, dec_all, rope, s_f, s_b, need_out, prev):
    bsz, seq, rb0 = seg
    hd = RET_HEAD_DIM
    hps = RET_HPS
    wide = hps * hd
    use_rope = rope is not None
    col = lambda off: pl.BlockSpec((seq, wide), lambda bi, h: (rb0 + bi, off // wide + h))
    state = pl.BlockSpec((1, hps, hd, hd), lambda bi, h: (bi, h, 0, 0))
    in_specs = [pl.BlockSpec((None, hps, 2, hd), lambda bi, h: (layer, h, 0, 0)),
                col(O_Q), col(O_K), col(O_V), col(O_G)]
    args = [dec_all, z, z, z, z]
    if use_rope:
        in_specs += [pl.BlockSpec((seq, LANE), lambda bi, h: (0, 0))] * 2
        args += list(rope)
    in_specs += [state, state]
    args += [s_f, s_b]
    out_specs = [state, state]
    out_shape = [jax.ShapeDtypeStruct((bsz, RET_HEADS, hd, hd), F32)] * 2
    n_chunks = seq // RET_BLOCK
    scratch = [pltpu.VMEM((seq, wide), BF16), pltpu.VMEM((n_chunks, hps, hd, RET_BLOCK), BF16),
               pltpu.VMEM((seq, wide), F32), pltpu.VMEM((seq, wide), F32),
               pltpu.VMEM((hps, 2, hd, hd), F32)]
    kw = dict(grid=(bsz, RET_HEADS // hps), scratch_shapes=scratch,
              compiler_params=_params(("parallel", "parallel"), 56))
    kern = functools.partial(_ret_kernel, use_rope=use_rope, need_out=need_out)
    if not need_out:
        res = pl.pallas_call(kern, in_specs=in_specs, out_specs=out_specs, out_shape=out_shape,
                             name="retention_state", **kw)(*args)
        return None, res[0], res[1]
    out_specs = [pl.BlockSpec((seq, wide), lambda bi, h: (rb0 + bi, h))] + out_specs
    out_shape = [jax.ShapeDtypeStruct((z.shape[0], D_RET), BF16)] + out_shape
    res = _seg_call(kern, prev=prev, n_in=len(args), in_specs=in_specs, out_specs=out_specs,
                    out_shape=out_shape, name="retention", args=args, **kw)
    return res[0], res[1], res[2]


def _rope_tables(seq):
    rows = seq // GRID_W
    row = jnp.repeat(jnp.arange(rows, dtype=F32), GRID_W)
    colp = jnp.tile(jnp.arange(GRID_W, dtype=F32), rows)
    inv = ROPE_BASE ** (-jnp.arange(ROPE_PAIRS, dtype=F32) / ROPE_PAIRS)
    ang_r = row[:, None] * inv[None, :]
    ang_c = colp[:, None] * inv[None, :]
    return (jnp.concatenate([jnp.cos(ang_r), jnp.cos(ang_c)], axis=-1),
            jnp.concatenate([jnp.sin(ang_r), jnp.sin(ang_c)], axis=-1))


def _merge_kernel(ya_ref, yb_ref, yc_ref, ga_ref, gb_ref, gc_ref, pa_ref, pb_ref, pc_ref, o_ref):
    def branch(g_ref, y_ref, p_ref, rows):
        return (jax.nn.sigmoid(g_ref[rows, :].astype(F32))
                * jnp.dot(y_ref[rows, :], p_ref[...], preferred_element_type=F32))

    sub = o_ref.shape[0] // MERGE_SUB
    for s in range(MERGE_SUB):
        rows = slice(s * sub, (s + 1) * sub)
        m = (branch(ga_ref, ya_ref, pa_ref, rows) + branch(gb_ref, yb_ref, pb_ref, rows)
             + branch(gc_ref, yc_ref, pc_ref, rows))
        o_ref[rows, :] = m.astype(BF16)


def _merge(ya, yb, yc, z, p_a, p_b, p_c, m_rows):
    tm, tn = TM_MERGE, TN_WS
    goff = O_GATE // tn
    gstep = D_MODEL // tn
    act = lambda width: pl.BlockSpec((tm, width), lambda i, j: (i, 0))
    gate = lambda br: pl.BlockSpec((tm, tn), lambda i, j: (i, goff + br * gstep + j))
    wgt = lambda rows: pl.BlockSpec((rows, tn), lambda i, j: (0, j))
    return pl.pallas_call(
        _merge_kernel,
        grid=(m_rows // tm, D_MODEL // tn),
        in_specs=[act(D_POOL), act(D_HYENA), act(D_RET), gate(0), gate(1), gate(2),
                  wgt(D_POOL), wgt(D_HYENA), wgt(D_RET)],
        out_specs=pl.BlockSpec((tm, tn), lambda i, j: (i, j)),
        out_shape=jax.ShapeDtypeStruct((m_rows, D_MODEL), BF16),
        compiler_params=_params(("parallel", "arbitrary"), 48),
        name="merge",
    )(ya, yb, yc, z, z, z, p_a, p_b, p_c)


def _res_ln_kernel(*refs, emit_xm, n_k):
    a_ref, w_ref, b_ref, x_ref, gt_ref, g_ref, be_ref = refs[:7]
    refs = refs[7:]
    if emit_xm:
        sh_ref, sc_ref, xo_ref, xm_ref, acc_a, acc_b, row_ref = refs
    else:
        xo_ref, acc_a, acc_b, row_ref = refs
    k = pl.program_id(1)
    sub = x_ref.shape[0] // RES_SUB

    def spread_rows():
        tile = lambda v: jnp.broadcast_to(v, (LN_ROWS, v.shape[1]))
        gate = gt_ref[0] * (1.0 / DEEPNORM_ALPHA)
        row_ref[0] = tile(gate)
        row_ref[1] = tile(gate * b_ref[0])
        row_ref[2] = tile(g_ref[0])
        row_ref[3] = tile(be_ref[0])
        if emit_xm:
            row_ref[4] = tile(g_ref[0] * (1.0 + sc_ref[0]))
            row_ref[5] = tile(be_ref[0] * (1.0 + sc_ref[0]) + sh_ref[0])

    def body(acc_ref):
        def finish(r0):
            rr = slice(r0, r0 + LN_ROWS)
            r = x_ref[rr, :] + row_ref[0] * acc_ref[rr, :] + row_ref[1]
            mu = jnp.mean(r, axis=-1, keepdims=True)
            d = r - mu
            var = jnp.mean(d * d, axis=-1, keepdims=True)
            xhat = d * lax.rsqrt(var + LN_EPS / DEEPNORM_ALPHA ** 2)
            xo_ref[rr, :] = xhat * row_ref[2] + row_ref[3]
            if emit_xm:
                xm_ref[rr, :] = (xhat * row_ref[4] + row_ref[5]).astype(BF16)

        def step(first, last):
            if last:
                spread_rows()
            for s in range(RES_SUB):
                rows = slice(s * sub, (s + 1) * sub)
                part = jnp.dot(a_ref[rows, :], w_ref[...], preferred_element_type=F32)
                if first:
                    acc_ref[rows, :] = part
                else:
                    acc_ref[rows, :] += part
                if last:
                    for r0 in range(s * sub, (s + 1) * sub, LN_ROWS):
                        finish(r0)

        if n_k == 1:
            step(True, True)
        else:
            pl.when(k == 0)(lambda: step(True, False))
            pl.when(jnp.logical_and(k > 0, k < n_k - 1))(lambda: step(False, False))
            pl.when(k == n_k - 1)(lambda: step(False, True))

    pl.when(pl.program_id(2) == 0)(lambda: body(acc_a))
    pl.when(pl.program_id(2) == 1)(lambda: body(acc_b))


def _res_ln(a, w, b_all, x, mod, layer, gate_blk, ln_g, ln_b, next_mod, m_rows, seq, bsz, name):
    kdim = a.shape[1]
    tm = TM_ROW
    tk = min(TK_ROW, kdim)
    n_k = kdim // tk
    n_tiles = m_rows // tm
    assert n_tiles % 2 == 0
    per = seq // tm
    gidx = _mod_index(layer, per, bsz)
    vec = lambda arr: arr.reshape(DEPTH, 1, D_MODEL)
    lvec = pl.BlockSpec((1, 1, D_MODEL), lambda p, k, r: (layer, 0, 0))
    tile_of = lambda p, k, r: jnp.where(k == n_k - 1, 2 * p + r, 2 * p)
    in_specs = [pl.BlockSpec((tm, tk), lambda p, k, r: (2 * p + r, k)),
                pl.BlockSpec((tk, D_MODEL), lambda p, k, r: (k, 0)),
                lvec,
                pl.BlockSpec((tm, D_MODEL), lambda p, k, r: (tile_of(p, k, r), 0)),
                pl.BlockSpec((1, 1, D_MODEL), lambda p, k, r: (gidx(tile_of(p, k, r)), 0, gate_blk)),
                lvec, lvec]
    args = [a, w, vec(b_all), x, mod, vec(ln_g), vec(ln_b)]
    row_out = pl.BlockSpec((tm, D_MODEL), lambda p, k, r: (tile_of(p, k, r), 0))
    out_specs = [row_out]
    out_shape = [jax.ShapeDtypeStruct((m_rows, D_MODEL), F32)]
    if next_mod is not None:
        nl, sh_blk, sc_blk = next_mod
        nidx = _mod_index(nl, per, bsz)
        in_specs += [pl.BlockSpec((1, 1, D_MODEL), lambda p, k, r: (nidx(tile_of(p, k, r)), 0, sh_blk)),
                     pl.BlockSpec((1, 1, D_MODEL), lambda p, k, r: (nidx(tile_of(p, k, r)), 0, sc_blk))]
        args += [mod, mod]
        out_specs.append(row_out)
        out_shape.append(jax.ShapeDtypeStruct((m_rows, D_MODEL), BF16))
    res = pl.pallas_call(
        functools.partial(_res_ln_kernel, emit_xm=next_mod is not None, n_k=n_k),
        grid=(n_tiles // 2, n_k, 2),
        in_specs=in_specs,
        out_specs=out_specs,
        out_shape=out_shape,
        scratch_shapes=[pltpu.VMEM((tm, D_MODEL), F32), pltpu.VMEM((tm, D_MODEL), F32),
                        pltpu.VMEM((6, LN_ROWS, D_MODEL), F32)],
        compiler_params=_params(("parallel", "arbitrary", "arbitrary"), 58),
        name=name,
    )(*args)
    return (res[0], res[1]) if next_mod is not None else (res[0], None)


def kernel(x, c, ctx, c_ctx, w_ada, b_ada, w_in, b_in, conv_w, conv_b, pool_w, pool_scale, filt_w1, filt_b1, filt_f1, filt_w2, filt_b2, filt_f2, filt_w3, filt_b3, filt_f3, filt_w4, hyena_d, ret_decay, p_a, p_b, p_c, w_o, b_o, ln1_g, ln1_b, w_mlp1, b_mlp1, w_mlp2, b_mlp2, ln2_g, ln2_b):
    bsz, seq, _ = x.shape
    ctx_len = ctx.shape[1]
    assert x.shape == (bsz, seq, D_MODEL) and ctx.shape == (bsz, ctx_len, D_MODEL)
    assert seq % RET_BLOCK == 0 and ctx_len % RET_BLOCK == 0 and seq % GRID_W == 0
    assert bsz + 1 <= ADA_ROWS and seq % ctx_len == 0
    n_lat, n_ctx = bsz * seq, bsz * ctx_len
    m_total = n_lat + n_ctx
    assert seq % TM_MERGE == 0 and n_ctx % TM_MERGE == 0 and seq % TM_ROW == 0 and n_ctx % TM_ROW == 0
    assert O_Q % TN_WS == 0 and O_V % TN_WS == 0
    seg_x = (bsz, seq, 0)
    seg_c = (bsz, ctx_len, n_lat // ctx_len)

    cvec = jnp.concatenate([c, c_ctx[None, :], jnp.zeros((ADA_ROWS - bsz - 1, D_MODEL), F32)], axis=0)
    mod = _ada(cvec, w_ada, b_ada).reshape(DEPTH * ADA_ROWS, 1, 6 * D_MODEL)

    deltas = _filter_deltas()
    dft_x, dft_c = _dft_matrices(seq), _dft_matrices(ctx_len)
    zfeat_x, zfeat_c = _filter_features(seq), _filter_features(ctx_len)
    rope = _rope_tables(seq)
    zero_state = jnp.zeros((bsz, RET_HEADS, RET_HEAD_DIM, RET_HEAD_DIM), F32)

    row3 = lambda a: a.reshape(DEPTH, 1, -1)
    fp = {'w1': jnp.pad(filt_w1, ((0, 0), (0, FEAT_PAD - FILTER_EMB), (0, 0))),
          'b1': row3(filt_b1), 'f1': row3(filt_f1), 'w2': filt_w2, 'b2': row3(filt_b2), 'f2': row3(filt_f2),
          'w3': filt_w3, 'b3': row3(filt_b3), 'f3': row3(filt_f3), 'w4': filt_w4}
    dec_all = jnp.broadcast_to(jnp.swapaxes(ret_decay, 1, 2)[:, :, :, None],
                               (DEPTH, RET_HEADS, 2, RET_HEAD_DIM))
    kspec_x = _filter_spectrum(dft_x[0], _filters(zfeat_x, deltas, fp))
    kspec_c = _filter_spectrum(dft_c[0], _filters(zfeat_c, deltas, fp))
    xs, xm = _mod0(x.reshape(n_lat, D_MODEL), ctx.reshape(n_ctx, D_MODEL), mod, seq, bsz)

    def hyena(z, seg, l, dft, kspec, prev):
        fwd, inv = dft
        uu, x0 = _hy_pre(z, seg, l, conv_w, conv_b)
        return _hy_inv(fwd, inv, _hy_fwd(fwd, uu, kspec, l, hyena_d), x0, seg, m_total, prev)

    ya = jnp.zeros((m_total, D_POOL), BF16)
    yb = jnp.zeros((m_total, D_HYENA), BF16)
    yc = jnp.zeros((m_total, D_RET), BF16)
    qk_tiles = tuple(range(O_Q // TN_WS, O_V // TN_WS))
    for l in range(DEPTH):
        last = l == DEPTH - 1
        rows = n_lat if last else m_total
        z, wo_b, pa_b, pb_b, pc_b = _wsmm(xm, w_in, b_in, l, m_total, False, "in_proj",
                                          sides=(w_o, p_a, p_b, p_c), regroup_tiles=qk_tiles)
        if not last:
            ya = _pool(z, seg_c, l, pool_w, pool_scale, ya)
            yb = hyena(z, seg_c, l, dft_c, kspec_c, yb)
            yc, s_f, s_b = _retention(z, seg_c, l, dec_all, None, zero_state, zero_state, True, yc)
        else:
            _, s_f, s_b = _retention(z, seg_c, l, dec_all, None, zero_state, zero_state, False, None)
        ya = _pool(z, seg_x, l, pool_w, pool_scale, ya)
        yb = hyena(z, seg_x, l, dft_x, kspec_x, yb)
        yc, _, _ = _retention(z, seg_x, l, dec_all, rope, s_f, s_b, True, yc)
        merged = _merge(ya, yb, yc, z, pa_b, pb_b, pc_b, rows)
        xs, xm = _res_ln(merged, wo_b, b_o, xs, mod, l, 2, ln1_g, ln1_b, (l, 3, 4), rows, seq, bsz,
                         "out_proj_ln1")
        hid, w2_b = _wsmm(xm, w_mlp1, b_mlp1, l, rows, True, "mlp_up", sides=(w_mlp2,))
        xs, xm = _res_ln(hid, w2_b, b_mlp2, xs, mod, l, 5, ln2_g, ln2_b,
                         None if last else (l + 1, 0, 1), rows, seq, bsz, "mlp_down_ln2")
    return xs.reshape(bsz, seq, D_MODEL)
```
